```python
import math
import jax, jax.numpy as jnp
from jax import lax
import numpy as np

D_MODEL = 2048
BATCH = 2
SEQ = 8192
DEPTH = 2

MEM_LEN = 256
N_EVEN = (DEPTH + 1) // 2
N_ODD = DEPTH // 2
WIDTH_A = D_MODEL // 2
WIDTH_B = D_MODEL // 2
POOL_WINDOWS = (2, 4, 8, 16)
N_POOL_GROUPS = len(POOL_WINDOWS)
POOL_GROUP = WIDTH_A // N_POOL_GROUPS
HGRN_HEAD_DIM = 128
HGRN_HEADS = WIDTH_B // HGRN_HEAD_DIM
HGRN_CHUNK = 32
SB_HEAD_DIM = 128
SB_HEADS = D_MODEL // SB_HEAD_DIM
SB_BLOCK = 128
XA_HEADS = 4
XA_HEAD_DIM = D_MODEL // XA_HEADS
EVEN_IN = 2 * WIDTH_A + 4 * WIDTH_B
ODD_IN = 4 * D_MODEL
EPS = 1e-6

kernel_name = 'hybrid_pool_hgrn2_stickbreak_block'


def rms_norm(x, g):
    xf = x.astype(jnp.float32)
    y = xf * lax.rsqrt(jnp.mean(xf * xf, axis=-1, keepdims=True) + EPS)
    return (y * g.astype(jnp.float32)).astype(x.dtype)


def pool_mixer(u, w, scale):
    B, S, _ = u.shape
    ug = u.astype(jnp.float32).reshape(B, S, N_POOL_GROUPS, POOL_GROUP)
    cs = jnp.cumsum(ug, axis=1)
    pooled = []
    for gi, win in enumerate(POOL_WINDOWS):
        c = cs[:, :, gi]
        shifted = jnp.pad(c, ((0, 0), (win, 0), (0, 0)))[:, :S]
        cnt = jnp.minimum(jnp.arange(1, S + 1), win).astype(jnp.float32)
        pooled.append((c - shifted) / cnt[None, :, None])
    mixed = jnp.stack(pooled, axis=2) - ug
    y = jnp.einsum('bsgc,gcd->bsgd', mixed, w.astype(jnp.float32))
    y = y.reshape(B, S, WIDTH_A) * scale.astype(jnp.float32)
    return y.astype(u.dtype)


def hgrn2(q, f_logit, inp, lb, norm_g):
    B, S, _ = q.shape
    H, D, C = HGRN_HEADS, HGRN_HEAD_DIM, HGRN_CHUNK
    N = S // C
    f32 = jnp.float32
    f = lb + (1.0 - lb) * jax.nn.sigmoid(f_logit.astype(f32))
    logf = jnp.log(f)
    k = 1.0 - f

    def heads(t):
        return t.astype(f32).reshape(B, N, C, H, D).transpose(0, 3, 1, 2, 4)

    qh, kh, ih, lfh = heads(q), heads(k), heads(inp), heads(logf)
    bcum = jnp.cumsum(lfh, axis=3)
    q_t = qh * jnp.exp(bcum)
    k_t = kh * jnp.exp(-bcum)
    causal = jnp.tril(jnp.ones((C, C), dtype=bool))
    scores = jnp.where(causal, jnp.einsum('bhncd,bhnsd->bhncs', q_t, k_t), 0.0)
    o_intra = jnp.einsum('bhncs,bhnsv->bhncv', scores, ih)
    b_last = bcum[:, :, :, -1]
    k_end = kh * jnp.exp(b_last[:, :, :, None, :] - bcum)
    U = jnp.einsum('bhnsd,bhnsv->bhndv', k_end, ih)
    decay = jnp.exp(b_last)

    def step(state, xs):
        d, u = xs
        return d[..., None] * state + u, state

    s0 = jnp.zeros((B, H, D, D), f32)
    _, s_prev = lax.scan(step, s0, (decay.transpose(2, 0, 1, 3), U.transpose(2, 0, 1, 3, 4)))
    s_prev = s_prev.transpose(1, 2, 0, 3, 4)
    o = o_intra + jnp.einsum('bhncd,bhndv->bhncv', q_t, s_prev)
    o = o.transpose(0, 2, 3, 1, 4).reshape(B, S, H, D)
    o = o * lax.rsqrt(jnp.mean(o * o, axis=-1, keepdims=True) + EPS)
    o = o * norm_g.astype(f32).reshape(H, D)
    return o.reshape(B, S, WIDTH_B).astype(q.dtype)


def stick_breaking_attention(q, k, v):
    S = q.shape[2]
    scale = 1.0 / math.sqrt(SB_HEAD_DIM)
    outs = []
    for blk in range(S // SB_BLOCK):
        t0, t1 = blk * SB_BLOCK, (blk + 1) * SB_BLOCK
        qb, kb, vb = q[:, :, t0:t1], k[:, :, :t1], v[:, :, :t1]
        z = jnp.einsum('bhtd,bhsd->bhts', qb, kb).astype(jnp.float32) * scale
        mask = jnp.arange(t1)[None, :] < jnp.arange(t0, t1)[:, None]
        log_beta = jax.nn.log_sigmoid(z)
        log_keep = jnp.where(mask, jax.nn.log_sigmoid(-z), 0.0)
        rev = jnp.cumsum(log_keep[..., ::-1], axis=-1)[..., ::-1] - log_keep
        A = jnp.where(mask, jnp.exp(log_beta + rev), 0.0)
        outs.append(jnp.einsum('bhts,bhsd->bhtd', A.astype(vb.dtype), vb))
    return jnp.concatenate(outs, axis=2)


def cross_attention(h, m, wq, wkv, wo):
    B, S, _ = h.shape
    M = m.shape[1]
    q = (h @ wq).reshape(B, S, XA_HEADS, XA_HEAD_DIM)
    k, v = jnp.split(m @ wkv, 2, axis=-1)
    k = k.reshape(B, M, XA_HEADS, XA_HEAD_DIM)
    v = v.reshape(B, M, XA_HEADS, XA_HEAD_DIM)
    s = jnp.einsum('bshd,bmhd->bhsm', q, k).astype(jnp.float32) / math.sqrt(XA_HEAD_DIM)
    p = jax.nn.softmax(s, axis=-1).astype(v.dtype)
    o = jnp.einsum('bhsm,bmhd->bshd', p, v).reshape(B, S, D_MODEL)
    return o @ wo


def setup_inputs(seed: int = 0) -> dict:
    key = jax.random.key(seed)
    ks = jax.random.split(key, 20)
    f32 = jnp.float32
    nrm = lambda k, shape, s: jax.random.normal(k, shape, f32) * s
    return {
        'x': nrm(ks[0], (BATCH, SEQ, D_MODEL), 1.0),
        'mem': nrm(ks[1], (BATCH, MEM_LEN, D_MODEL), 1.0),
        'norm_mix_g': 1.0 + nrm(ks[2], (DEPTH, D_MODEL), 0.02),
        'norm_xa_g': 1.0 + nrm(ks[3], (DEPTH, D_MODEL), 0.02),
        'norm_mem_g': 1.0 + nrm(ks[4], (DEPTH, D_MODEL), 0.02),
        'final_norm_g': 1.0 + nrm(ks[5], (D_MODEL,), 0.02),
        'w_in_even': nrm(ks[6], (N_EVEN, D_MODEL, EVEN_IN), D_MODEL ** -0.5),
        'pool_w': nrm(ks[7], (N_EVEN, N_POOL_GROUPS, POOL_GROUP, POOL_GROUP), POOL_GROUP ** -0.5),
        'pool_scale': 1.0 + nrm(ks[8], (N_EVEN, WIDTH_A), 0.02),
        'hgrn_lower_bounds': nrm(ks[9], (N_EVEN + 1, WIDTH_B), 0.1),
        'hgrn_norm_g': 1.0 + nrm(ks[10], (N_EVEN, WIDTH_B), 0.02),
        'w_out_even': nrm(ks[11], (N_EVEN, WIDTH_A + WIDTH_B, D_MODEL), (WIDTH_A + WIDTH_B) ** -0.5),
        'w_in_odd': nrm(ks[12], (N_ODD, D_MODEL, ODD_IN), D_MODEL ** -0.5),
        'w_out_odd': nrm(ks[13], (N_ODD, D_MODEL, D_MODEL), D_MODEL ** -0.5),
        'xa_wq': nrm(ks[14], (DEPTH, D_MODEL, D_MODEL), D_MODEL ** -0.5),
        'xa_wkv': nrm(ks[15], (DEPTH, D_MODEL, 2 * D_MODEL), D_MODEL ** -0.5),
        'xa_wo': nrm(ks[16], (DEPTH, D_MODEL, D_MODEL), D_MODEL ** -0.5),
    }


def reference(x, mem, norm_mix_g, norm_xa_g, norm_mem_g, final_norm_g,
              w_in_even, pool_w, pool_scale, hgrn_lower_bounds, hgrn_norm_g, w_out_even,
              w_in_odd, w_out_odd, xa_wq, xa_wkv, xa_wo):
    B, S, _ = x.shape
    lb_table = jnp.cumsum(jax.nn.softmax(hgrn_lower_bounds.astype(jnp.float32), axis=0), axis=0)
    for l in range(DEPTH):
        h = rms_norm(x, norm_mix_g[l])
        if l % 2 == 0:
            e = l // 2
            u = h @ w_in_even[e]
            a_in, a_gate, b_q, b_f, b_i, b_gate = jnp.split(u, 6, axis=-1)
            y_a = pool_mixer(a_in, pool_w[e], pool_scale[e]) * jax.nn.silu(a_gate)
            y_b = hgrn2(b_q, b_f, b_i, lb_table[e], hgrn_norm_g[e]) * jax.nn.silu(b_gate)
            x = x + jnp.concatenate([y_a, y_b], axis=-1) @ w_out_even[e]
        else:
            o = l // 2
            u = h @ w_in_odd[o]
            q, k, v, gate = jnp.split(u, 4, axis=-1)
            sh = lambda t: t.reshape(B, S, SB_HEADS, SB_HEAD_DIM).transpose(0, 2, 1, 3)
            att = stick_breaking_attention(sh(q), sh(k), sh(v))
            att = att.transpose(0, 2, 1, 3).reshape(B, S, D_MODEL)
            x = x + (att * jax.nn.silu(gate)) @ w_out_odd[o]
        hx = rms_norm(x, norm_xa_g[l])
        hm = rms_norm(mem, norm_mem_g[l])
        x = x + cross_attention(hx, hm, xa_wq[l], xa_wkv[l], xa_wo[l])
    return rms_norm(x, final_norm_g)
```

```python
import functools
import math

import jax
import jax.numpy as jnp
from jax import lax
from jax.experimental import pallas as pl
from jax.experimental.pallas import tpu as pltpu

F32 = jnp.float32
BF16 = jnp.bfloat16

EPS = 1e-6
POOL_WINDOWS = (2, 4, 8, 16)
POOL_HALO = 16
HGRN_HEAD_DIM = 128
HGRN_CHUNK = 32
SB_HEAD_DIM = 128
XA_HEADS = 4

V7X_VMEM_BYTES = 64 * 1024 * 1024
VMEM_CAP_BYTES = V7X_VMEM_BYTES - 8 * 1024 * 1024
COMPILER_SCRATCH_BYTES = 8 * 1024 * 1024

SB_DEAD_LOG_WEIGHT = -104.0


def _vmem_limit(pipelined_bytes, resident_bytes=0):
    need = 2 * pipelined_bytes + resident_bytes + COMPILER_SCRATCH_BYTES
    return int(min(VMEM_CAP_BYTES, need))


def _nbytes(shape, dtype):
    return math.prod(shape) * jnp.dtype(dtype).itemsize


def _sigmoid(x):
    return 1.0 / (1.0 + jnp.exp(-x))


def _silu(x):
    return x * _sigmoid(x)


def _split_bf16(x):
    hi = x.astype(BF16)
    lo = (x - hi.astype(F32)).astype(BF16)
    return hi, lo


NORM_ROWS = 256


def _norm_matmul_kernel(x_ref, g_ref, w_ref, o_ref, h_ref):
    tm = x_ref.shape[0]

    @pl.when(pl.program_id(1) == 0)
    def _():
        g = g_ref[...]

        def body(r, carry):
            rows = pl.ds(pl.multiple_of(r * NORM_ROWS, NORM_ROWS), NORM_ROWS)
            x = x_ref[rows, :]
            ms = jnp.mean(x * x, axis=-1, keepdims=True)
            h_ref[rows, :] = (x * lax.rsqrt(ms + EPS) * g).astype(h_ref.dtype)
            return carry

        lax.fori_loop(0, tm // NORM_ROWS, body, 0)

    o_ref[...] = jnp.dot(h_ref[...], w_ref[...],
                         preferred_element_type=F32).astype(o_ref.dtype)


def _norm_matmul(x, g, w, out_dtype, tm, tn):
    m, k = x.shape
    n = w.shape[1]
    assert m % tm == 0 and n % tn == 0 and tm % NORM_ROWS == 0
    blocks = (_nbytes((tm, k), x.dtype) + _nbytes((k, tn), w.dtype)
              + _nbytes((tm, tn), out_dtype))
    scratch = _nbytes((tm, k), BF16) + _nbytes((tm, tn), F32)
    return pl.pallas_call(
        _norm_matmul_kernel,
        grid=(m // tm, n // tn),
        in_specs=[
            pl.BlockSpec((tm, k), lambda i, j: (i, 0)),
            pl.BlockSpec((1, k), lambda i, j: (0, 0)),
            pl.BlockSpec((k, tn), lambda i, j: (0, j)),
        ],
        out_specs=pl.BlockSpec((tm, tn), lambda i, j: (i, j)),
        out_shape=jax.ShapeDtypeStruct((m, n), out_dtype),
        scratch_shapes=[pltpu.VMEM((tm, k), BF16)],
        compiler_params=pltpu.CompilerParams(
            dimension_semantics=("arbitrary", "arbitrary"),
            vmem_limit_bytes=_vmem_limit(blocks, scratch)),
        name="norm_matmul",
    )(x, g.reshape(1, k), w)


def _matmul_residual_kernel(*refs, n_lhs):
    lhs_refs = refs[:n_lhs]
    w_refs = refs[n_lhs:2 * n_lhs]
    res_ref, o_ref = refs[2 * n_lhs], refs[2 * n_lhs + 1]
    acc = res_ref[...]
    for a_ref, w_ref in zip(lhs_refs, w_refs):
        acc = acc + jnp.dot(a_ref[...], w_ref[...], preferred_element_type=F32)
    o_ref[...] = acc


def _matmul_residual(lhs_list, w, res, tm, tn):
    m, n = res.shape
    kp = lhs_list[0].shape[1]
    assert all(a.shape == (m, kp) for a in lhs_list)
    assert w.shape == (kp * len(lhs_list), n)
    n_lhs = len(lhs_list)
    in_specs = [pl.BlockSpec((tm, kp), lambda i, j: (i, 0)) for _ in lhs_list]
    in_specs += [pl.BlockSpec((kp, tn), functools.partial(lambda i, j, p: (p, j), p=p))
                 for p in range(n_lhs)]
    in_specs += [pl.BlockSpec((tm, tn), lambda i, j: (i, j))]
    blocks = (n_lhs * (_nbytes((tm, kp), BF16) + _nbytes((kp, tn), BF16))
              + 2 * _nbytes((tm, tn), F32))
    return pl.pallas_call(
        functools.partial(_matmul_residual_kernel, n_lhs=n_lhs),
        grid=(m // tm, n // tn),
        in_specs=in_specs,
        out_specs=pl.BlockSpec((tm, tn), lambda i, j: (i, j)),
        out_shape=jax.ShapeDtypeStruct((m, n), F32),
        compiler_params=pltpu.CompilerParams(
            dimension_semantics=("arbitrary", "arbitrary"),
            vmem_limit_bytes=_vmem_limit(blocks, _nbytes((tm, tn), F32))),
        name="matmul_residual",
    )(*lhs_list, *([w] * n_lhs), res)


def _rmsnorm_kernel(x_ref, g_ref, o_ref):
    x = x_ref[...]
    ms = jnp.mean(x * x, axis=-1, keepdims=True)
    o_ref[...] = x * lax.rsqrt(ms + EPS) * g_ref[...]


def _rmsnorm(x, g, tm):
    m, k = x.shape
    return pl.pallas_call(
        _rmsnorm_kernel,
        grid=(m // tm,),
        in_specs=[pl.BlockSpec((tm, k), lambda i: (i, 0)),
                  pl.BlockSpec((1, k), lambda i: (0, 0))],
        out_specs=pl.BlockSpec((tm, k), lambda i: (i, 0)),
        out_shape=jax.ShapeDtypeStruct((m, k), F32),
        compiler_params=pltpu.CompilerParams(
            dimension_semantics=("arbitrary",),
            vmem_limit_bytes=_vmem_limit(2 * _nbytes((tm, k), F32),
                                         _nbytes((tm, k), F32))),
        name="final_rmsnorm",
    )(x, g.reshape(1, k))


def _pool_kernel(a_ref, gate_ref, w_ref, scale_ref, o_ref, buf_ref, *, group):
    t = a_ref.shape[1]
    sblk = pl.program_id(1)

    @pl.when(sblk == 0)
    def _():
        buf_ref[0:POOL_HALO, :] = jnp.zeros((POOL_HALO, buf_ref.shape[1]), F32)

    buf_ref[POOL_HALO:POOL_HALO + t, :] = a_ref[0]
    pos = sblk * t + lax.broadcasted_iota(jnp.int32, (t, 1), 0)
    for gi, win in enumerate(POOL_WINDOWS):
        cols = slice(gi * group, (gi + 1) * group)
        cur = buf_ref[POOL_HALO:POOL_HALO + t, cols]
        tot = cur
        for back in range(1, win):
            tot = tot + buf_ref[POOL_HALO - back:POOL_HALO - back + t, cols]
        cnt = jnp.minimum(pos + 1, win).astype(F32)
        mixed = tot / cnt - cur
        y = jnp.dot(mixed.astype(BF16), w_ref[gi], preferred_element_type=F32)
        y = y * scale_ref[:, cols]
        o_ref[0, :, cols] = (y * _silu(gate_ref[0, :, cols])).astype(o_ref.dtype)
    buf_ref[0:POOL_HALO, :] = buf_ref[t:t + POOL_HALO, :]


def _pool_mixer(u, pool_w, pool_scale, t):
    b, s, _ = u.shape
    n_groups, group, _ = pool_w.shape
    width = n_groups * group
    blocks = 2 * _nbytes((t, width), F32) + _nbytes((t, width), BF16)
    resident = (_nbytes((t + POOL_HALO, width), F32) + _nbytes(pool_w.shape, BF16)
                + 4 * _nbytes((t, group), F32))
    return pl.pallas_call(
        functools.partial(_pool_kernel, group=group),
        grid=(b, s // t),
        in_specs=[
            pl.BlockSpec((1, t, width), lambda bi, si: (bi, si, 0)),
            pl.BlockSpec((1, t, width), lambda bi, si: (bi, si, 1)),
            pl.BlockSpec(pool_w.shape, lambda bi, si: (0, 0, 0)),
            pl.BlockSpec((1, width), lambda bi, si: (0, 0)),
        ],
        out_specs=pl.BlockSpec((1, t, width), lambda bi, si: (bi, si, 0)),
        out_shape=jax.ShapeDtypeStruct((b, s, width), BF16),
        scratch_shapes=[pltpu.VMEM((t + POOL_HALO, width), F32)],
        compiler_params=pltpu.CompilerParams(
            dimension_semantics=("arbitrary", "arbitrary"),
            vmem_limit_bytes=_vmem_limit(blocks, resident)),
        name="pool_mixer",
    )(u, u, pool_w, pool_scale.reshape(1, width))


def _hgrn_kernel(q_ref, f_ref, i_ref, gate_ref, lbp_ref, ng_ref, tril_ref, ones_ref,
                 o_ref, state_ref, *, layer):
    t = q_ref.shape[1]
    n_chunks = t // HGRN_CHUNK

    @pl.when(pl.program_id(2) == 0)
    def _():
        state_ref[...] = jnp.zeros(state_ref.shape, F32)

    lbp = lbp_ref[...]
    ex = jnp.exp(lbp - jnp.max(lbp, axis=0, keepdims=True))
    lb = jnp.sum(ex[:layer + 1], axis=0, keepdims=True) / jnp.sum(ex, axis=0, keepdims=True)

    q = q_ref[0]
    inp = i_ref[0]
    f = lb + (1.0 - lb) * _sigmoid(f_ref[0])
    logf = jnp.log(f)
    k = 1.0 - f

    tril = tril_ref[...]
    same = ones_ref[...]
    hi, lo = _split_bf16(logf)
    bcum = (jnp.dot(tril, hi, preferred_element_type=F32)
            + jnp.dot(tril, lo, preferred_element_type=F32))
    blast = (jnp.dot(same, hi, preferred_element_type=F32)
             + jnp.dot(same, lo, preferred_element_type=F32))

    q_t = (q * jnp.exp(bcum)).astype(BF16)
    k_t = (k * jnp.exp(-bcum)).astype(BF16)
    k_end = (k * jnp.exp(blast - bcum)).astype(BF16)
    decay = jnp.exp(blast)
    inp_b = inp.astype(BF16)

    scores = lax.dot_general(q_t, k_t, (((1,), (1,)), ((), ())),
                             preferred_element_type=F32)
    scores = jnp.where(tril > 0, scores, 0.0).astype(BF16)
    o_intra = jnp.dot(scores, inp_b, preferred_element_type=F32)

    state_t = state_ref[...]
    o_inter = []
    for n in range(n_chunks):
        rows = slice(n * HGRN_CHUNK, (n + 1) * HGRN_CHUNK)
        o_inter.append(lax.dot_general(q_t[rows], state_t.astype(BF16),
                                       (((1,), (1,)), ((), ())),
                                       preferred_element_type=F32))
        u_t = lax.dot_general(inp_b[rows], k_end[rows], (((0,), (0,)), ((), ())),
                              preferred_element_type=F32)
        state_t = state_t * decay[n * HGRN_CHUNK:n * HGRN_CHUNK + 1, :] + u_t
    state_ref[...] = state_t

    o = o_intra + jnp.concatenate(o_inter, axis=0)
    o = o * lax.rsqrt(jnp.mean(o * o, axis=-1, keepdims=True) + EPS) * ng_ref[...]
    o_ref[0] = (o * _silu(gate_ref[0])).astype(o_ref.dtype)


def _hgrn2(u, lower_bounds, norm_g, layer, t):
    b, s, six_w = u.shape
    width = six_w // 6
    d = HGRN_HEAD_DIM
    heads = width // d
    chunk_id = jnp.arange(t, dtype=jnp.int32) // HGRN_CHUNK
    same = chunk_id[:, None] == chunk_id[None, :]
    tril = same & (jnp.arange(t)[None, :] <= jnp.arange(t)[:, None])
    col = lambda base: (lambda bi, hi, si: (bi, si, base * heads + hi))
    blocks = 4 * _nbytes((t, d), F32) + _nbytes((t, d), BF16)
    resident = 2 * _nbytes((t, t), BF16) + 16 * _nbytes((t, d), F32) + 2 * _nbytes((t, t), F32)
    return pl.pallas_call(
        functools.partial(_hgrn_kernel, layer=layer),
        grid=(b, heads, s // t),
        in_specs=[
            pl.BlockSpec((1, t, d), col(2)),
            pl.BlockSpec((1, t, d), col(3)),
            pl.BlockSpec((1, t, d), col(4)),
            pl.BlockSpec((1, t, d), col(5)),
            pl.BlockSpec((lower_bounds.shape[0], d), lambda bi, hi, si: (0, hi)),
            pl.BlockSpec((1, d), lambda bi, hi, si: (0, hi)),
            pl.BlockSpec((t, t), lambda bi, hi, si: (0, 0)),
            pl.BlockSpec((t, t), lambda bi, hi, si: (0, 0)),
        ],
        out_specs=pl.BlockSpec((1, t, d), lambda bi, hi, si: (bi, si, hi)),
        out_shape=jax.ShapeDtypeStruct((b, s, width), BF16),
        scratch_shapes=[pltpu.VMEM((d, d), F32)],
        compiler_params=pltpu.CompilerParams(
            dimension_semantics=("arbitrary", "arbitrary", "arbitrary"),
            vmem_limit_bytes=_vmem_limit(blocks, resident)),
        name="hgrn2",
    )(u, u, u, u, lower_bounds, norm_g.reshape(1, width),
      tril.astype(BF16), same.astype(BF16))


def _sb_kernel(q_ref, k_ref, v_ref, gate_ref, m_ref, o_ref, *, scale):
    tq = q_ref.shape[1]
    d = q_ref.shape[2]
    qi = pl.program_id(2)
    q = q_ref[0]
    m_ext = m_ref[...]

    def tile(j, carry, acc, diagonal):
        start = pl.multiple_of(j * tq, tq)
        kb = k_ref[0, pl.ds(start, tq), :]
        vb = v_ref[0, pl.ds(start, tq), :]
        z = lax.dot_general(q, kb, (((1,), (1,)), ((), ())),
                            preferred_element_type=F32) * scale
        soft = jnp.log(1.0 + jnp.exp(-jnp.abs(z)))
        log_beta = -(jnp.maximum(-z, 0.0) + soft)
        log_keep = -(jnp.maximum(z, 0.0) + soft)
        if diagonal:
            mask = (lax.broadcasted_iota(jnp.int32, (tq, tq), 1)
                    < lax.broadcasted_iota(jnp.int32, (tq, tq), 0))
            log_keep = jnp.where(mask, log_keep, 0.0)
        hi, lo = _split_bf16(log_keep)
        sums = (jnp.dot(hi, m_ext, preferred_element_type=F32)
                + jnp.dot(lo, m_ext, preferred_element_type=F32))
        later = sums[:, :tq]
        total = sums[:, tq:]
        rev = later + jnp.concatenate([carry] * (tq // d), axis=1)
        a = jnp.exp(log_beta + rev)
        if diagonal:
            a = jnp.where(mask, a, 0.0)
        acc = acc + jnp.dot(a.astype(BF16), vb, preferred_element_type=F32)
        return carry + total, acc

    carry, acc = tile(qi, jnp.zeros((tq, d), F32), jnp.zeros((tq, d), F32), True)

    def cond(state):
        j, carry, _ = state
        return jnp.logical_and(j >= 0, jnp.max(carry) > SB_DEAD_LOG_WEIGHT)

    def body(state):
        j, carry, acc = state
        carry, acc = tile(j, carry, acc, False)
        return j - 1, carry, acc

    _, _, acc = lax.while_loop(cond, body, (qi - 1, carry, acc))
    o_ref[0] = (acc * _silu(gate_ref[0].astype(F32))).astype(o_ref.dtype)


def _stick_breaking(u, tq):
    b, s, four_d = u.shape
    dm = four_d // 4
    d = SB_HEAD_DIM
    heads = dm // d
    row = jnp.arange(tq)
    later = (row[:, None] > row[None, :]).astype(BF16)
    m_ext = jnp.concatenate([later, jnp.ones((tq, d), BF16)], axis=1)
    blocks = 2 * _nbytes((s, d), BF16) + 3 * _nbytes((tq, d), BF16)
    resident = _nbytes(m_ext.shape, BF16) + 12 * _nbytes((tq, tq), F32)
    return pl.pallas_call(
        functools.partial(_sb_kernel, scale=1.0 / math.sqrt(d)),
        grid=(b, heads, s // tq),
        in_specs=[
            pl.BlockSpec((1, tq, d), lambda bi, hi, qi: (bi, qi, hi)),
            pl.BlockSpec((1, s, d), lambda bi, hi, qi: (bi, 0, heads + hi)),
            pl.BlockSpec((1, s, d), lambda bi, hi, qi: (bi, 0, 2 * heads + hi)),
            pl.BlockSpec((1, tq, d), lambda bi, hi, qi: (bi, qi, 3 * heads + hi)),
            pl.BlockSpec(m_ext.shape, lambda bi, hi, qi: (0, 0)),
        ],
        out_specs=pl.BlockSpec((1, tq, d), lambda bi, hi, qi: (bi, qi, hi)),
        out_shape=jax.ShapeDtypeStruct((b, s, dm), BF16),
        compiler_params=pltpu.CompilerParams(
            dimension_semantics=("arbitrary", "arbitrary", "arbitrary"),
            vmem_limit_bytes=_vmem_limit(blocks, resident)),
        name="stick_breaking",
    )(u, u, u, u, m_ext)


def _xattn_kernel(q_ref, k_ref, v_ref, o_ref, *, heads):
    dm = q_ref.shape[2]
    hd = dm // heads
    scale = 1.0 / math.sqrt(hd)
    for h in range(heads):
        cols = slice(h * hd, (h + 1) * hd)
        s = lax.dot_general(q_ref[0, :, cols], k_ref[0, :, cols],
                            (((1,), (1,)), ((), ())),
                            preferred_element_type=F32) * scale
        e = jnp.exp(s - jnp.max(s, axis=-1, keepdims=True))
        p = e / jnp.sum(e, axis=-1, keepdims=True)
        o_ref[0, :, cols] = jnp.dot(p.astype(BF16), v_ref[0, :, cols],
                                    preferred_element_type=F32).astype(o_ref.dtype)


def _cross_attention_core(q, kv, t):
    b, s, dm = q.shape
    mlen = kv.shape[1]
    blocks = 2 * _nbytes((t, dm), BF16) + 2 * _nbytes((mlen, dm), BF16)
    resident = 6 * _nbytes((t, mlen), F32) + 2 * _nbytes((t, dm // XA_HEADS), F32)
    return pl.pallas_call(
        functools.partial(_xattn_kernel, heads=XA_HEADS),
        grid=(b, s // t),
        in_specs=[
            pl.BlockSpec((1, t, dm), lambda bi, si: (bi, si, 0)),
            pl.BlockSpec((1, mlen, dm), lambda bi, si: (bi, 0, 0)),
            pl.BlockSpec((1, mlen, dm), lambda bi, si: (bi, 0, 1)),
        ],
        out_specs=pl.BlockSpec((1, t, dm), lambda bi, si: (bi, si, 0)),
        out_shape=jax.ShapeDtypeStruct((b, s, dm), BF16),
        compiler_params=pltpu.CompilerParams(
            dimension_semantics=("arbitrary", "arbitrary"),
            vmem_limit_bytes=_vmem_limit(blocks, resident)),
        name="cross_attention",
    )(q, kv, kv)


MM_TM = 1024
MM_TN = 1024
POOL_ROWS = 512
HGRN_ROWS = 256
SB_ROWS = 256
XA_ROWS = 512


def kernel(x, mem, norm_mix_g, norm_xa_g, norm_mem_g, final_norm_g, w_in_even, pool_w,
           pool_scale, hgrn_lower_bounds, hgrn_norm_g, w_out_even, w_in_odd, w_out_odd,
           xa_wq, xa_wkv, xa_wo):
    b, s, dm = x.shape
    mlen = mem.shape[1]
    depth = norm_mix_g.shape[0]
    rows = b * s
    xs = x.reshape(rows, dm)
    mem2 = mem.reshape(b * mlen, dm)
    for l in range(depth):
        if l % 2 == 0:
            e = l // 2
            u = _norm_matmul(xs, norm_mix_g[l], w_in_even[e].astype(BF16), F32, MM_TM, MM_TN)
            u = u.reshape(b, s, -1)
            y_a = _pool_mixer(u, pool_w[e].astype(BF16), pool_scale[e], POOL_ROWS)
            y_b = _hgrn2(u, hgrn_lower_bounds, hgrn_norm_g[e], e, HGRN_ROWS)
            xs = _matmul_residual([y_a.reshape(rows, -1), y_b.reshape(rows, -1)],
                                  w_out_even[e].astype(BF16), xs, MM_TM, MM_TN)
        else:
            o = l // 2
            u = _norm_matmul(xs, norm_mix_g[l], w_in_odd[o].astype(BF16), BF16, MM_TM, MM_TN)
            y = _stick_breaking(u.reshape(b, s, -1), SB_ROWS)
            xs = _matmul_residual([y.reshape(rows, dm)], w_out_odd[o].astype(BF16), xs,
                                  MM_TM, MM_TN)
        q = _norm_matmul(xs, norm_xa_g[l], xa_wq[l].astype(BF16), BF16, MM_TM, MM_TN)
        kv = _norm_matmul(mem2, norm_mem_g[l], xa_wkv[l].astype(BF16), BF16,
                          b * mlen, MM_TN)
        att = _cross_attention_core(q.reshape(b, s, dm), kv.reshape(b, mlen, 2 * dm), XA_ROWS)
        xs = _matmul_residual([att.reshape(rows, dm)], xa_wo[l].astype(BF16), xs, MM_TM, MM_TN)
    return _rmsnorm(xs, final_norm_g, MM_TM).reshape(b, s, dm)
```

```python
import functools
import math

import jax
import jax.numpy as jnp
from jax import lax
from jax.experimental import pallas as pl
from jax.experimental.pallas import tpu as pltpu

F32 = jnp.float32
BF16 = jnp.bfloat16

EPS = 1e-6
LOG2E = math.log2(math.e)
POOL_WINDOWS = (2, 4, 8, 16)
POOL_HALO = 16
HGRN_HEAD_DIM = 128
HGRN_CHUNK = 32
BF16_SUBLANES = 16
SB_HEAD_DIM = 128
XA_HEADS = 4

V7X_VMEM_BYTES = 64 * 1024 * 1024
VMEM_CAP_BYTES = V7X_VMEM_BYTES - 8 * 1024 * 1024
COMPILER_SCRATCH_BYTES = 8 * 1024 * 1024

SB_DEAD_LOG2_WEIGHT = -151.0


def _vmem_limit(pipelined_bytes, resident_bytes=0):
    need = 2 * pipelined_bytes + resident_bytes + COMPILER_SCRATCH_BYTES
    return int(min(VMEM_CAP_BYTES, need))


def _nbytes(shape, dtype):
    return math.prod(shape) * jnp.dtype(dtype).itemsize


def _sigmoid(x):
    return 1.0 / (1.0 + jnp.exp(-x))


def _silu(x):
    return x * _sigmoid(x)


def _split_bf16(x):
    hi = x.astype(BF16)
    lo = (x - hi.astype(F32)).astype(BF16)
    return hi, lo


def _rms_scale(x):
    return x * lax.rsqrt(jnp.mean(x * x, axis=-1, keepdims=True) + EPS)


NORM_ROWS = 256


def _norm_matmul_kernel(x_ref, g_ref, w_ref, o_ref, h_ref):
    tm = x_ref.shape[0]

    @pl.when(pl.program_id(1) == 0)
    def _():
        g = g_ref[...]

        def body(r, carry):
            rows = pl.ds(pl.multiple_of(r * NORM_ROWS, NORM_ROWS), NORM_ROWS)
            h_ref[rows, :] = (_rms_scale(x_ref[rows, :]) * g).astype(h_ref.dtype)
            return carry

        lax.fori_loop(0, tm // NORM_ROWS, body, 0)

    o_ref[...] = jnp.dot(h_ref[...], w_ref[...],
                         preferred_element_type=F32).astype(o_ref.dtype)


def _norm_matmul(x, g, w, out_dtype, tm, tn):
    m, k = x.shape
    n = w.shape[1]
    assert m % tm == 0 and n % tn == 0 and tm % NORM_ROWS == 0
    blocks = (_nbytes((tm, k), x.dtype) + _nbytes((k, tn), w.dtype)
              + _nbytes((tm, tn), out_dtype))
    scratch = _nbytes((tm, k), BF16) + _nbytes((tm, tn), F32)
    return pl.pallas_call(
        _norm_matmul_kernel,
        grid=(m // tm, n // tn),
        in_specs=[
            pl.BlockSpec((tm, k), lambda i, j: (i, 0)),
            pl.BlockSpec((1, k), lambda i, j: (0, 0)),
            pl.BlockSpec((k, tn), lambda i, j: (0, j)),
        ],
        out_specs=pl.BlockSpec((tm, tn), lambda i, j: (i, j)),
        out_shape=jax.ShapeDtypeStruct((m, n), out_dtype),
        scratch_shapes=[pltpu.VMEM((tm, k), BF16)],
        compiler_params=pltpu.CompilerParams(
            dimension_semantics=("arbitrary", "arbitrary"),
            vmem_limit_bytes=_vmem_limit(blocks, scratch)),
        name="norm_matmul",
    )(x, g.reshape(1, k), w)


def _matmul_residual_kernel(*refs, n_lhs, final_norm):
    lhs_refs = refs[:n_lhs]
    w_refs = refs[n_lhs:2 * n_lhs]
    res_ref = refs[2 * n_lhs]
    o_ref = refs[-1]
    acc = res_ref[...]
    for a_ref, w_ref in zip(lhs_refs, w_refs):
        acc = acc + jnp.dot(a_ref[...], w_ref[...], preferred_element_type=F32)
    if final_norm:
        acc = _rms_scale(acc) * refs[2 * n_lhs + 1][...]
    o_ref[...] = acc


def _matmul_residual(lhs_list, w, res, tm, final_g=None):
    m, n = res.shape
    kp = lhs_list[0].shape[1]
    assert all(a.shape == (m, kp) for a in lhs_list)
    assert w.shape == (kp * len(lhs_list), n)
    n_lhs = len(lhs_list)
    in_specs = [pl.BlockSpec((tm, kp), lambda i: (i, 0)) for _ in lhs_list]
    in_specs += [pl.BlockSpec((kp, n), functools.partial(lambda i, p: (p, 0), p=p))
                 for p in range(n_lhs)]
    in_specs += [pl.BlockSpec((tm, n), lambda i: (i, 0))]
    args = [*lhs_list, *([w] * n_lhs), res]
    if final_g is not None:
        in_specs += [pl.BlockSpec((1, n), lambda i: (0, 0))]
        args += [final_g.reshape(1, n)]
    blocks = (n_lhs * (_nbytes((tm, kp), BF16) + _nbytes((kp, n), BF16))
              + 2 * _nbytes((tm, n), F32))
    return pl.pallas_call(
        functools.partial(_matmul_residual_kernel, n_lhs=n_lhs,
                          final_norm=final_g is not None),
        grid=(m // tm,),
        in_specs=in_specs,
        out_specs=pl.BlockSpec((tm, n), lambda i: (i, 0)),
        out_shape=jax.ShapeDtypeStruct((m, n), F32),
        compiler_params=pltpu.CompilerParams(
            dimension_semantics=("arbitrary",),
            vmem_limit_bytes=_vmem_limit(blocks, 2 * _nbytes((tm, n), F32))),
        name="matmul_residual",
    )(*args)


def _pool_kernel(a_ref, gate_ref, w_ref, scale_ref, o_ref, buf_ref, *, group):
    t = a_ref.shape[1]
    sblk = pl.program_id(1)

    @pl.when(sblk == 0)
    def _():
        buf_ref[0:POOL_HALO, :] = jnp.zeros((POOL_HALO, buf_ref.shape[1]), F32)

    buf_ref[POOL_HALO:POOL_HALO + t, :] = a_ref[0]
    pos = sblk * t + lax.broadcasted_iota(jnp.int32, (t, 1), 0)
    for gi, win in enumerate(POOL_WINDOWS):
        cols = slice(gi * group, (gi + 1) * group)
        cur = buf_ref[POOL_HALO:POOL_HALO + t, cols]
        tot = cur
        for back in range(1, win):
            tot = tot + buf_ref[POOL_HALO - back:POOL_HALO - back + t, cols]
        cnt = jnp.minimum(pos + 1, win).astype(F32)
        mixed = tot / cnt - cur
        y = jnp.dot(mixed.astype(BF16), w_ref[gi], preferred_element_type=F32)
        y = y * scale_ref[:, cols]
        o_ref[0, :, cols] = (y * _silu(gate_ref[0, :, cols])).astype(o_ref.dtype)
    buf_ref[0:POOL_HALO, :] = buf_ref[t:t + POOL_HALO, :]


def _pool_mixer(u, pool_w, pool_scale, t):
    b, s, _ = u.shape
    n_groups, group, _ = pool_w.shape
    width = n_groups * group
    blocks = 2 * _nbytes((t, width), F32) + _nbytes((t, width), BF16)
    resident = (_nbytes((t + POOL_HALO, width), F32) + _nbytes(pool_w.shape, BF16)
                + 4 * _nbytes((t, group), F32))
    return pl.pallas_call(
        functools.partial(_pool_kernel, group=group),
        grid=(b, s // t),
        in_specs=[
            pl.BlockSpec((1, t, width), lambda bi, si: (bi, si, 0)),
            pl.BlockSpec((1, t, width), lambda bi, si: (bi, si, 1)),
            pl.BlockSpec(pool_w.shape, lambda bi, si: (0, 0, 0)),
            pl.BlockSpec((1, width), lambda bi, si: (0, 0)),
        ],
        out_specs=pl.BlockSpec((1, t, width), lambda bi, si: (bi, si, 0)),
        out_shape=jax.ShapeDtypeStruct((b, s, width), BF16),
        scratch_shapes=[pltpu.VMEM((t + POOL_HALO, width), F32)],
        compiler_params=pltpu.CompilerParams(
            dimension_semantics=("arbitrary", "arbitrary"),
            vmem_limit_bytes=_vmem_limit(blocks, resident)),
        name="pool_mixer",
    )(u, u, pool_w, pool_scale.reshape(1, width))


def _hgrn_kernel(q_ref, f_ref, i_ref, gate_ref, lbp_ref, ng_ref, cum_ref,
                 o_ref, state_ref, *, layer, heads_per_step):
    t = q_ref.shape[1]
    d = HGRN_HEAD_DIM
    gw = heads_per_step * d
    n_chunks = t // HGRN_CHUNK
    nt_dims = (((1,), (1,)), ((), ()))
    tn_dims = (((0,), (0,)), ((), ()))

    @pl.when(pl.program_id(2) == 0)
    def _():
        state_ref[...] = jnp.zeros(state_ref.shape, F32)

    lbp = lbp_ref[...]
    ex = jnp.exp(lbp - jnp.max(lbp, axis=0, keepdims=True))
    lb = jnp.sum(ex[:layer + 1], axis=0, keepdims=True) / jnp.sum(ex, axis=0, keepdims=True)

    f = lb + (1.0 - lb) * _sigmoid(f_ref[0])
    logf = jnp.log(f)
    k = 1.0 - f

    cum_m = cum_ref[...]
    hi, lo = _split_bf16(logf)
    sums = jnp.dot(cum_m, jnp.concatenate([hi, lo], axis=1), preferred_element_type=F32)
    sums = sums[:, :gw] + sums[:, gw:]
    bcum = sums[:t]
    decay = jnp.exp(sums[t:])

    q_t = (q_ref[0] * jnp.exp(bcum)).astype(BF16)
    k_t = k * jnp.exp(-bcum)
    k_t_b = k_t.astype(BF16)
    inp_b = i_ref[0].astype(BF16)
    in_chunk = cum_m[:t] > 0
    heads = [slice(h * d, (h + 1) * d) for h in range(heads_per_step)]
    chunks = [slice(n * HGRN_CHUNK, (n + 1) * HGRN_CHUNK) for n in range(n_chunks)]

    scores = [lax.dot_general(q_t[:, c], k_t_b[:, c], nt_dims, preferred_element_type=F32)
              for c in heads]
    scores = [jnp.where(in_chunk, s, 0.0).astype(BF16) for s in scores]
    o_intra = [jnp.dot(s, inp_b[:, c], preferred_element_type=F32)
               for s, c in zip(scores, heads)]

    k_end = [(k_t[r] * decay[n:n + 1, :]).astype(BF16) for n, r in enumerate(chunks)]
    incr = [[lax.dot_general(inp_b[r, c], k_end[n][:, c], tn_dims, preferred_element_type=F32)
             for c in heads] for n, r in enumerate(chunks)]
    states = []
    cur = [state_ref[h] for h in range(heads_per_step)]
    for n in range(n_chunks):
        states.append([s.astype(BF16) for s in cur])
        cur = [s * decay[n:n + 1, c] + u for s, c, u in zip(cur, heads, incr[n])]
    for h in range(heads_per_step):
        state_ref[h] = cur[h]

    o_inter = [jnp.concatenate(
        [lax.dot_general(q_t[r, c], states[n][h], nt_dims, preferred_element_type=F32)
         for n, r in enumerate(chunks)], axis=0) for h, c in enumerate(heads)]

    for h, c in enumerate(heads):
        o = _rms_scale(o_intra[h] + o_inter[h]) * ng_ref[:, c]
        o_ref[0, :, c] = (o * _silu(gate_ref[0, :, c])).astype(o_ref.dtype)


def _hgrn2(u, lower_bounds, norm_g, layer, t, heads_per_step):
    b, s, six_w = u.shape
    width = six_w // 6
    d = HGRN_HEAD_DIM
    gw = heads_per_step * d
    groups = width // gw
    pos = jnp.arange(t, dtype=jnp.int32)
    chunk_id = pos // HGRN_CHUNK
    tril = (chunk_id[:, None] == chunk_id[None, :]) & (pos[None, :] <= pos[:, None])
    select = jnp.arange(BF16_SUBLANES, dtype=jnp.int32)[:, None] == chunk_id[None, :]
    assert t // HGRN_CHUNK <= BF16_SUBLANES
    cum_m = jnp.concatenate([tril, select], axis=0).astype(BF16)
    col = lambda base: (lambda bi, gi, si: (bi, si, base * groups + gi))
    blocks = 4 * _nbytes((t, gw), F32) + _nbytes((t, gw), BF16)
    resident = (2 * _nbytes(cum_m.shape, BF16) + _nbytes((heads_per_step, d, d), F32)
                + heads_per_step * (16 * _nbytes((t, d), F32) + 2 * _nbytes((t, t), F32)))
    return pl.pallas_call(
        functools.partial(_hgrn_kernel, layer=layer, heads_per_step=heads_per_step),
        grid=(b, groups, s // t),
        in_specs=[
            pl.BlockSpec((1, t, gw), col(2)),
            pl.BlockSpec((1, t, gw), col(3)),
            pl.BlockSpec((1, t, gw), col(4)),
            pl.BlockSpec((1, t, gw), col(5)),
            pl.BlockSpec((lower_bounds.shape[0], gw), lambda bi, gi, si: (0, gi)),
            pl.BlockSpec((1, gw), lambda bi, gi, si: (0, gi)),
            pl.BlockSpec(cum_m.shape, lambda bi, gi, si: (0, 0)),
        ],
        out_specs=pl.BlockSpec((1, t, gw), lambda bi, gi, si: (bi, si, gi)),
        out_shape=jax.ShapeDtypeStruct((b, s, width), BF16),
        scratch_shapes=[pltpu.VMEM((heads_per_step, d, d), F32)],
        compiler_params=pltpu.CompilerParams(
            dimension_semantics=("arbitrary", "arbitrary", "arbitrary"),
            vmem_limit_bytes=_vmem_limit(blocks, resident)),
        name="hgrn2",
    )(u, u, u, u, lower_bounds, norm_g.reshape(1, width), cum_m)


def _sb_tile(q, kb, vb, later_m, carry, acc, mask):
    z = lax.dot_general(q, kb, (((1,), (1,)), ((), ())), preferred_element_type=F32)
    soft = jnp.log(1.0 + jnp.exp2(-jnp.abs(z))) * LOG2E
    log_beta = jnp.minimum(z, 0.0) - soft
    log_keep = log_beta - z
    if mask is not None:
        log_keep = jnp.where(mask, log_keep, 0.0)
    later = jnp.dot(log_keep.astype(BF16), later_m, preferred_element_type=F32)
    a = jnp.exp2(log_beta + later + carry)
    if mask is not None:
        a = jnp.where(mask, a, 0.0)
    acc = acc + jnp.dot(a.astype(BF16), vb, preferred_element_type=F32)
    return carry + jnp.sum(log_keep, axis=-1, keepdims=True), acc


def _sb_kernel(q_ref, k_ref, v_ref, gate_ref, m_ref, o_ref, *, n_sub, tk):
    d = q_ref.shape[2]
    qi = pl.program_id(2)
    later_m = m_ref[...]
    causal = (lax.broadcasted_iota(jnp.int32, (tk, tk), 1)
              < lax.broadcasted_iota(jnp.int32, (tk, tk), 0))

    def kv_block(j):
        rows = pl.ds(pl.multiple_of(j * tk, tk), tk)
        return k_ref[0, rows, :], v_ref[0, rows, :]

    def write(s, acc):
        rows = slice(s * tk, (s + 1) * tk)
        o_ref[0, rows, :] = (acc * _silu(gate_ref[0, rows, :].astype(F32))).astype(o_ref.dtype)

    def step(sequence_start):
        state = []
        for s in range(n_sub):
            g = qi * n_sub + s
            q = q_ref[0, s * tk:(s + 1) * tk, :]
            carry, acc = _sb_tile(q, *kv_block(g), later_m, jnp.zeros((tk, 1), F32),
                                  jnp.zeros((tk, d), F32), causal)
            nxt = g - 1
            if not (sequence_start and s == 0):
                carry, acc = _sb_tile(q, *kv_block(g - 1), later_m, carry, acc, None)
                nxt = g - 2
            write(s, acc)
            state.append((q, nxt, carry, acc))

        worst = functools.reduce(jnp.maximum, [c for _, _, c, _ in state])

        @pl.when(jnp.max(worst) > SB_DEAD_LOG2_WEIGHT)
        def _():
            for s, (q, nxt, carry, acc) in enumerate(state):
                def cond(st):
                    j, alive, _, _ = st
                    return jnp.logical_and(j >= 0, alive)

                def body(st, q=q):
                    j, _, carry, acc = st
                    carry, acc = _sb_tile(q, *kv_block(j), later_m, carry, acc, None)
                    return j - 1, jnp.max(carry) > SB_DEAD_LOG2_WEIGHT, carry, acc

                alive = jnp.max(carry) > SB_DEAD_LOG2_WEIGHT
                _, _, _, acc = lax.while_loop(cond, body, (nxt, alive, carry, acc))
                write(s, acc)

    @pl.when(qi == 0)
    def _():
        step(True)

    @pl.when(qi > 0)
    def _():
        step(False)


def _stick_breaking(u, tq, tk):
    b, s, four_d = u.shape
    dm = four_d // 4
    d = SB_HEAD_DIM
    heads = dm // d
    row = jnp.arange(tk)
    later_m = (row[:, None] > row[None, :]).astype(BF16)
    blocks = 2 * _nbytes((s, d), BF16) + 3 * _nbytes((tq, d), BF16)
    resident = _nbytes(later_m.shape, BF16) + (tq // tk) * 12 * _nbytes((tk, tk), F32)
    return pl.pallas_call(
        functools.partial(_sb_kernel, n_sub=tq // tk, tk=tk),
        grid=(b, heads, s // tq),
        in_specs=[
            pl.BlockSpec((1, tq, d), lambda bi, hi, qi: (bi, qi, hi)),
            pl.BlockSpec((1, s, d), lambda bi, hi, qi: (bi, 0, heads + hi)),
            pl.BlockSpec((1, s, d), lambda bi, hi, qi: (bi, 0, 2 * heads + hi)),
            pl.BlockSpec((1, tq, d), lambda bi, hi, qi: (bi, qi, 3 * heads + hi)),
            pl.BlockSpec(later_m.shape, lambda bi, hi, qi: (0, 0)),
        ],
        out_specs=pl.BlockSpec((1, tq, d), lambda bi, hi, qi: (bi, qi, hi)),
        out_shape=jax.ShapeDtypeStruct((b, s, dm), BF16),
        compiler_params=pltpu.CompilerParams(
            dimension_semantics=("arbitrary", "arbitrary", "arbitrary"),
            vmem_limit_bytes=_vmem_limit(blocks, resident)),
        name="stick_breaking",
    )(u, u, u, u, later_m)


def _xattn_kernel(q_ref, k_ref, v_ref, o_ref, *, heads):
    dm = q_ref.shape[2]
    hd = dm // heads
    scale = 1.0 / math.sqrt(hd)
    for h in range(heads):
        cols = slice(h * hd, (h + 1) * hd)
        s = lax.dot_general(q_ref[0, :, cols], k_ref[0, :, cols],
                            (((1,), (1,)), ((), ())),
                            preferred_element_type=F32) * scale
        e = jnp.exp(s - jnp.max(s, axis=-1, keepdims=True))
        p = e / jnp.sum(e, axis=-1, keepdims=True)
        o_ref[0, :, cols] = jnp.dot(p.astype(BF16), v_ref[0, :, cols],
                                    preferred_element_type=F32).astype(o_ref.dtype)


def _cross_attention_core(q, kv, t):
    b, s, dm = q.shape
    mlen = kv.shape[1]
    blocks = 2 * _nbytes((t, dm), BF16) + 2 * _nbytes((mlen, dm), BF16)
    resident = 6 * _nbytes((t, mlen), F32) + 2 * _nbytes((t, dm // XA_HEADS), F32)
    return pl.pallas_call(
        functools.partial(_xattn_kernel, heads=XA_HEADS),
        grid=(b, s // t),
        in_specs=[
            pl.BlockSpec((1, t, dm), lambda bi, si: (bi, si, 0)),
            pl.BlockSpec((1, mlen, dm), lambda bi, si: (bi, 0, 0)),
            pl.BlockSpec((1, mlen, dm), lambda bi, si: (bi, 0, 1)),
        ],
        out_specs=pl.BlockSpec((1, t, dm), lambda bi, si: (bi, si, 0)),
        out_shape=jax.ShapeDtypeStruct((b, s, dm), BF16),
        compiler_params=pltpu.CompilerParams(
            dimension_semantics=("arbitrary", "arbitrary"),
            vmem_limit_bytes=_vmem_limit(blocks, resident)),
        name="cross_attention",
    )(q, kv, kv)


MM_TM = 1024
MM_TN = 1024
RES_TM = 512
POOL_ROWS = 512
HGRN_ROWS = 256
HGRN_HEADS_PER_STEP = 4
SB_Q_ROWS = 512
SB_K_ROWS = 256
XA_ROWS = 512


def kernel(x, mem, norm_mix_g, norm_xa_g, norm_mem_g, final_norm_g, w_in_even, pool_w,
           pool_scale, hgrn_lower_bounds, hgrn_norm_g, w_out_even, w_in_odd, w_out_odd,
           xa_wq, xa_wkv, xa_wo):
    b, s, dm = x.shape
    mlen = mem.shape[1]
    depth = norm_mix_g.shape[0]
    rows = b * s
    xs = x.reshape(rows, dm)
    mem2 = mem.reshape(b * mlen, dm)
    sb_col_scale = jnp.concatenate([jnp.full((dm,), LOG2E / math.sqrt(SB_HEAD_DIM), F32),
                                    jnp.ones((3 * dm,), F32)])
    for l in range(depth):
        if l % 2 == 0:
            e = l // 2
            u = _norm_matmul(xs, norm_mix_g[l], w_in_even[e].astype(BF16), F32, MM_TM, MM_TN)
            u = u.reshape(b, s, -1)
            y_a = _pool_mixer(u, pool_w[e].astype(BF16), pool_scale[e], POOL_ROWS)
            y_b = _hgrn2(u, hgrn_lower_bounds, hgrn_norm_g[e], e, HGRN_ROWS,
                         HGRN_HEADS_PER_STEP)
            xs = _matmul_residual([y_a.reshape(rows, -1), y_b.reshape(rows, -1)],
                                  w_out_even[e].astype(BF16), xs, RES_TM)
        else:
            o = l // 2
            w_in = (w_in_odd[o] * sb_col_scale).astype(BF16)
            u = _norm_matmul(xs, norm_mix_g[l], w_in, BF16, MM_TM, MM_TN)
            y = _stick_breaking(u.reshape(b, s, -1), SB_Q_ROWS, SB_K_ROWS)
            xs = _matmul_residual([y.reshape(rows, dm)], w_out_odd[o].astype(BF16), xs, RES_TM)
        q = _norm_matmul(xs, norm_xa_g[l], xa_wq[l].astype(BF16), BF16, MM_TM, MM_TN)
        kv = _norm_matmul(mem2, norm_mem_g[l], xa_wkv[l].astype(BF16), BF16,
                          b * mlen, MM_TN)
        att = _cross_attention_core(q.reshape(b, s, dm), kv.reshape(b, mlen, 2 * dm), XA_ROWS)
        xs = _matmul_residual([att.reshape(rows, dm)], xa_wo[l].astype(BF16), xs, RES_TM,
                              final_g=final_norm_g if l == depth - 1 else None)
    return xs.reshape(b, s, dm)
```

```python
import functools
import math

import jax
import jax.numpy as jnp
from jax import lax
from jax.experimental import pallas as pl
from jax.experimental.pallas import tpu as pltpu

F32 = jnp.float32
BF16 = jnp.bfloat16

EPS = 1e-6
LOG2E = math.log2(math.e)
POOL_WINDOWS = (2, 4, 8, 16)
POOL_HALO = 16
HGRN_HEAD_DIM = 128
HGRN_CHUNK = 32
BF16_SUBLANES = 16
SB_HEAD_DIM = 128
XA_HEADS = 4

V7X_VMEM_BYTES = 64 * 1024 * 1024
VMEM_CAP_BYTES = V7X_VMEM_BYTES - 8 * 1024 * 1024
COMPILER_SCRATCH_BYTES = 8 * 1024 * 1024

SB_DEAD_LOG2_WEIGHT = -151.0
SB_MASKED_LOGIT = -1e30
F32_SIGN_BIT = -2 ** 31


def _vmem_limit(pipelined_bytes, resident_bytes=0):
    need = 2 * pipelined_bytes + resident_bytes + COMPILER_SCRATCH_BYTES
    return int(min(VMEM_CAP_BYTES, need))


def _nbytes(shape, dtype):
    return math.prod(shape) * jnp.dtype(dtype).itemsize


def _sigmoid(x):
    return 1.0 / (1.0 + jnp.exp(-x))


def _silu(x):
    return x * _sigmoid(x)


def _split_bf16(x):
    hi = x.astype(BF16)
    lo = (x - hi.astype(F32)).astype(BF16)
    return hi, lo


def _rms_scale(x):
    return x * lax.rsqrt(jnp.mean(x * x, axis=-1, keepdims=True) + EPS)


NORM_ROWS = 256


def _norm_matmul_kernel(x_ref, g_ref, w_ref, o_ref, h_ref):
    tm = x_ref.shape[0]

    @pl.when(pl.program_id(1) == 0)
    def _():
        g = g_ref[...]

        def body(r, carry):
            rows = pl.ds(pl.multiple_of(r * NORM_ROWS, NORM_ROWS), NORM_ROWS)
            h_ref[rows, :] = (_rms_scale(x_ref[rows, :]) * g).astype(h_ref.dtype)
            return carry

        lax.fori_loop(0, tm // NORM_ROWS, body, 0)

    o_ref[...] = jnp.dot(h_ref[...], w_ref[...],
                         preferred_element_type=F32).astype(o_ref.dtype)


def _norm_matmul(x, g, w, out_dtype, tm, tn):
    m, k = x.shape
    n = w.shape[1]
    assert m % tm == 0 and n % tn == 0 and tm % NORM_ROWS == 0
    blocks = (_nbytes((tm, k), x.dtype) + _nbytes((k, tn), w.dtype)
              + _nbytes((tm, tn), out_dtype))
    scratch = _nbytes((tm, k), BF16) + _nbytes((tm, tn), F32)
    return pl.pallas_call(
        _norm_matmul_kernel,
        grid=(m // tm, n // tn),
        in_specs=[
            pl.BlockSpec((tm, k), lambda i, j: (i, 0)),
            pl.BlockSpec((1, k), lambda i, j: (0, 0)),
            pl.BlockSpec((k, tn), lambda i, j: (0, j)),
        ],
        out_specs=pl.BlockSpec((tm, tn), lambda i, j: (i, j)),
        out_shape=jax.ShapeDtypeStruct((m, n), out_dtype),
        scratch_shapes=[pltpu.VMEM((tm, k), BF16)],
        compiler_params=pltpu.CompilerParams(
            dimension_semantics=("arbitrary", "arbitrary"),
            vmem_limit_bytes=_vmem_limit(blocks, scratch)),
        name="norm_matmul",
    )(x, g.reshape(1, k), w)


def _matmul_kernel(a_ref, w_ref, o_ref):
    o_ref[...] = jnp.dot(a_ref[...], w_ref[...],
                         preferred_element_type=F32).astype(o_ref.dtype)


def _matmul(a, w, out_dtype, tm, tn):
    m, k = a.shape
    n = w.shape[1]
    assert m % tm == 0 and n % tn == 0
    blocks = (_nbytes((tm, k), a.dtype) + _nbytes((k, tn), w.dtype)
              + _nbytes((tm, tn), out_dtype))
    return pl.pallas_call(
        _matmul_kernel,
        grid=(m // tm, n // tn),
        in_specs=[pl.BlockSpec((tm, k), lambda i, j: (i, 0)),
                  pl.BlockSpec((k, tn), lambda i, j: (0, j))],
        out_specs=pl.BlockSpec((tm, tn), lambda i, j: (i, j)),
        out_shape=jax.ShapeDtypeStruct((m, n), out_dtype),
        compiler_params=pltpu.CompilerParams(
            dimension_semantics=("arbitrary", "arbitrary"),
            vmem_limit_bytes=_vmem_limit(blocks, _nbytes((tm, tn), F32))),
        name="matmul",
    )(a, w)


def _matmul_residual_kernel(*refs, n_lhs, norm_mode):
    lhs_refs = refs[:n_lhs]
    w_refs = refs[n_lhs:2 * n_lhs]
    res_ref, g_ref = refs[2 * n_lhs], refs[2 * n_lhs + 1]
    out_refs = refs[2 * n_lhs + 2:]
    acc = res_ref[...]
    for a_ref, w_ref in zip(lhs_refs, w_refs):
        acc = acc + jnp.dot(a_ref[...], w_ref[...], preferred_element_type=F32)
    normed = _rms_scale(acc) * g_ref[...]
    if norm_mode == "final":
        out_refs[0][...] = normed
    else:
        out_refs[0][...] = acc
        out_refs[1][...] = normed.astype(out_refs[1].dtype)


def _matmul_residual(lhs_list, w, res, g, tm, norm_mode):
    assert norm_mode in ("emit", "final")
    m, n = res.shape
    kp = lhs_list[0].shape[1]
    assert all(a.shape == (m, kp) for a in lhs_list)
    assert w.shape == (kp * len(lhs_list), n)
    n_lhs = len(lhs_list)
    row_block = pl.BlockSpec((tm, n), lambda i: (i, 0))
    in_specs = [pl.BlockSpec((tm, kp), lambda i: (i, 0)) for _ in lhs_list]
    in_specs += [pl.BlockSpec((kp, n), functools.partial(lambda i, p: (p, 0), p=p))
                 for p in range(n_lhs)]
    in_specs += [row_block, pl.BlockSpec((1, n), lambda i: (0, 0))]
    out_shape = [jax.ShapeDtypeStruct((m, n), F32)]
    if norm_mode == "emit":
        out_shape += [jax.ShapeDtypeStruct((m, n), BF16)]
    blocks = (n_lhs * (_nbytes((tm, kp), BF16) + _nbytes((kp, n), BF16))
              + 2 * _nbytes((tm, n), F32) + _nbytes((tm, n), BF16))
    return pl.pallas_call(
        functools.partial(_matmul_residual_kernel, n_lhs=n_lhs, norm_mode=norm_mode),
        grid=(m // tm,),
        in_specs=in_specs,
        out_specs=[row_block] * len(out_shape),
        out_shape=out_shape,
        compiler_params=pltpu.CompilerParams(
            dimension_semantics=("arbitrary",),
            vmem_limit_bytes=_vmem_limit(blocks, 2 * _nbytes((tm, n), F32))),
        name="matmul_residual",
    )(*lhs_list, *([w] * n_lhs), res, g.reshape(1, n))


def _pool_kernel(a_ref, gate_ref, w_ref, scale_ref, o_ref, buf_ref, *, group):
    t = a_ref.shape[1]
    sblk = pl.program_id(1)

    @pl.when(sblk == 0)
    def _():
        buf_ref[0:POOL_HALO, :] = jnp.zeros((POOL_HALO, buf_ref.shape[1]), F32)

    buf_ref[POOL_HALO:POOL_HALO + t, :] = a_ref[0]
    pos = sblk * t + lax.broadcasted_iota(jnp.int32, (t, 1), 0)
    for gi, win in enumerate(POOL_WINDOWS):
        cols = slice(gi * group, (gi + 1) * group)
        cur = buf_ref[POOL_HALO:POOL_HALO + t, cols]
        tot = cur
        for back in range(1, win):
            tot = tot + buf_ref[POOL_HALO - back:POOL_HALO - back + t, cols]
        cnt = jnp.minimum(pos + 1, win).astype(F32)
        mixed = tot / cnt - cur
        y = jnp.dot(mixed.astype(BF16), w_ref[gi], preferred_element_type=F32)
        y = y * scale_ref[:, cols]
        o_ref[0, :, cols] = (y * _silu(gate_ref[0, :, cols])).astype(o_ref.dtype)
    buf_ref[0:POOL_HALO, :] = buf_ref[t:t + POOL_HALO, :]


def _pool_mixer(u, pool_w, pool_scale, t):
    b, s, _ = u.shape
    n_groups, group, _ = pool_w.shape
    width = n_groups * group
    blocks = 2 * _nbytes((t, width), F32) + _nbytes((t, width), BF16)
    resident = (_nbytes((t + POOL_HALO, width), F32) + _nbytes(pool_w.shape, BF16)
                + 4 * _nbytes((t, group), F32))
    return pl.pallas_call(
        functools.partial(_pool_kernel, group=group),
        grid=(b, s // t),
        in_specs=[
            pl.BlockSpec((1, t, width), lambda bi, si: (bi, si, 0)),
            pl.BlockSpec((1, t, width), lambda bi, si: (bi, si, 1)),
            pl.BlockSpec(pool_w.shape, lambda bi, si: (0, 0, 0)),
            pl.BlockSpec((1, width), lambda bi, si: (0, 0)),
        ],
        out_specs=pl.BlockSpec((1, t, width), lambda bi, si: (bi, si, 0)),
        out_shape=jax.ShapeDtypeStruct((b, s, width), BF16),
        scratch_shapes=[pltpu.VMEM((t + POOL_HALO, width), F32)],
        compiler_params=pltpu.CompilerParams(
            dimension_semantics=("arbitrary", "arbitrary"),
            vmem_limit_bytes=_vmem_limit(blocks, resident)),
        name="pool_mixer",
    )(u, u, pool_w, pool_scale.reshape(1, width))


def _hgrn_kernel(q_ref, f_ref, i_ref, gate_ref, lbp_ref, ng_ref, cum_ref,
                 o_ref, state_ref, *, layer, heads_per_step):
    t = q_ref.shape[1]
    d = HGRN_HEAD_DIM
    gw = heads_per_step * d
    n_chunks = t // HGRN_CHUNK
    nt_dims = (((1,), (1,)), ((), ()))
    tn_dims = (((0,), (0,)), ((), ()))

    @pl.when(pl.program_id(2) == 0)
    def _():
        state_ref[...] = jnp.zeros(state_ref.shape, F32)

    lbp = lbp_ref[...]
    ex = jnp.exp(lbp - jnp.max(lbp, axis=0, keepdims=True))
    lb = jnp.sum(ex[:layer + 1], axis=0, keepdims=True) / jnp.sum(ex, axis=0, keepdims=True)

    f = lb + (1.0 - lb) * _sigmoid(f_ref[0])
    logf = jnp.log(f)
    k = 1.0 - f

    cum_m = cum_ref[...]
    blk = cum_m.shape[1]
    blk_chunks = blk // HGRN_CHUNK
    blocks = [slice(r, r + blk) for r in range(0, t, blk)]
    hi, lo = _split_bf16(logf)
    hilo = jnp.concatenate([hi, lo], axis=1)
    sums = [jnp.dot(cum_m, hilo[r], preferred_element_type=F32) for r in blocks]
    sums = [x[:, :gw] + x[:, gw:] for x in sums]
    bcum = jnp.concatenate([x[:blk] for x in sums], axis=0)
    decay = jnp.exp(jnp.concatenate([x[blk:blk + blk_chunks] for x in sums], axis=0))

    q_t = (q_ref[0] * jnp.exp(bcum)).astype(BF16)
    k_t = k * jnp.exp(-bcum)
    k_t_b = k_t.astype(BF16)
    inp_b = i_ref[0].astype(BF16)
    in_chunk = cum_m[:blk] > 0
    heads = [slice(h * d, (h + 1) * d) for h in range(heads_per_step)]
    chunks = [slice(n * HGRN_CHUNK, (n + 1) * HGRN_CHUNK) for n in range(n_chunks)]

    scores = [[lax.dot_general(q_t[r, c], k_t_b[r, c], nt_dims, preferred_element_type=F32)
               for r in blocks] for c in heads]
    scores = [[jnp.where(in_chunk, s, 0.0).astype(BF16) for s in per_head]
              for per_head in scores]
    o_intra = [jnp.concatenate([jnp.dot(s, inp_b[r, c], preferred_element_type=F32)
                                for s, r in zip(per_head, blocks)], axis=0)
               for per_head, c in zip(scores, heads)]

    k_end = [(k_t[r] * decay[n:n + 1, :]).astype(BF16) for n, r in enumerate(chunks)]
    incr = [[lax.dot_general(inp_b[r, c], k_end[n][:, c], tn_dims, preferred_element_type=F32)
             for c in heads] for n, r in enumerate(chunks)]
    states = []
    cur = [state_ref[h] for h in range(heads_per_step)]
    for n in range(n_chunks):
        states.append([s.astype(BF16) for s in cur])
        cur = [s * decay[n:n + 1, c] + u for s, c, u in zip(cur, heads, incr[n])]
    for h in range(heads_per_step):
        state_ref[h] = cur[h]

    o_inter = [jnp.concatenate(
        [lax.dot_general(q_t[r, c], states[n][h], nt_dims, preferred_element_type=F32)
         for n, r in enumerate(chunks)], axis=0) for h, c in enumerate(heads)]

    for h, c in enumerate(heads):
        o = _rms_scale(o_intra[h] + o_inter[h]) * ng_ref[:, c]
        o_ref[0, :, c] = (o * _silu(gate_ref[0, :, c])).astype(o_ref.dtype)


def _hgrn2(u, lower_bounds, norm_g, layer, t, heads_per_step):
    b, s, six_w = u.shape
    width = six_w // 6
    d = HGRN_HEAD_DIM
    gw = heads_per_step * d
    groups = width // gw
    blk = HGRN_BLOCK
    assert t % blk == 0 and blk // HGRN_CHUNK <= BF16_SUBLANES
    pos = jnp.arange(blk, dtype=jnp.int32)
    chunk_id = pos // HGRN_CHUNK
    tril = (chunk_id[:, None] == chunk_id[None, :]) & (pos[None, :] <= pos[:, None])
    select = jnp.arange(BF16_SUBLANES, dtype=jnp.int32)[:, None] == chunk_id[None, :]
    cum_m = jnp.concatenate([tril, select], axis=0).astype(BF16)
    col = lambda base: (lambda bi, gi, si: (bi, si, base * groups + gi))
    blocks = 4 * _nbytes((t, gw), F32) + _nbytes((t, gw), BF16)
    resident = (2 * _nbytes(cum_m.shape, BF16) + _nbytes((heads_per_step, d, d), F32)
                + heads_per_step * (16 * _nbytes((t, d), F32) + 2 * _nbytes((t, blk), F32)))
    return pl.pallas_call(
        functools.partial(_hgrn_kernel, layer=layer, heads_per_step=heads_per_step),
        grid=(b, groups, s // t),
        in_specs=[
            pl.BlockSpec((1, t, gw), col(2)),
            pl.BlockSpec((1, t, gw), col(3)),
            pl.BlockSpec((1, t, gw), col(4)),
            pl.BlockSpec((1, t, gw), col(5)),
            pl.BlockSpec((lower_bounds.shape[0], gw), lambda bi, gi, si: (0, gi)),
            pl.BlockSpec((1, gw), lambda bi, gi, si: (0, gi)),
            pl.BlockSpec(cum_m.shape, lambda bi, gi, si: (0, 0)),
        ],
        out_specs=pl.BlockSpec((1, t, gw), lambda bi, gi, si: (bi, si, gi)),
        out_shape=jax.ShapeDtypeStruct((b, s, width), BF16),
        scratch_shapes=[pltpu.VMEM((heads_per_step, d, d), F32)],
        compiler_params=pltpu.CompilerParams(
            dimension_semantics=("arbitrary", "arbitrary", "arbitrary"),
            vmem_limit_bytes=_vmem_limit(blocks, resident)),
        name="hgrn2",
    )(u, u, u, u, lower_bounds, norm_g.reshape(1, width), cum_m)


def _sb_tile(q, kb, vb, later_m, carry, acc, mask):
    z = lax.dot_general(q, kb, (((1,), (1,)), ((), ())), preferred_element_type=F32)
    if mask is not None:
        z = jnp.where(mask, z, SB_MASKED_LOGIT)
    neg_abs = lax.bitcast_convert_type(
        lax.bitcast_convert_type(z, jnp.int32) | F32_SIGN_BIT, F32)
    soft = jnp.log(1.0 + jnp.exp2(neg_abs)) * LOG2E
    log_beta = jnp.minimum(z, 0.0) - soft
    log_keep = log_beta - z
    later = jnp.dot(log_keep.astype(BF16), later_m, preferred_element_type=F32)
    a = jnp.exp2(log_beta + later + carry)
    acc = acc + jnp.dot(a.astype(BF16), vb, preferred_element_type=F32)
    return carry + jnp.sum(log_keep, axis=-1, keepdims=True), acc


def _sb_kernel(q_ref, k_ref, v_ref, gate_ref, m_ref, o_ref, *, n_sub, tk):
    d = q_ref.shape[2]
    qi = pl.program_id(2)
    later_m = m_ref[...]
    causal = (lax.broadcasted_iota(jnp.int32, (tk, tk), 1)
              < lax.broadcasted_iota(jnp.int32, (tk, tk), 0))

    def kv_block(j):
        rows = pl.ds(pl.multiple_of(j * tk, tk), tk)
        return k_ref[0, rows, :], v_ref[0, rows, :]

    def write(s, acc):
        rows = slice(s * tk, (s + 1) * tk)
        o_ref[0, rows, :] = (acc * _silu(gate_ref[0, rows, :].astype(F32))).astype(o_ref.dtype)

    def step(sequence_start):
        state = []
        for s in range(n_sub):
            g = qi * n_sub + s
            q = q_ref[0, s * tk:(s + 1) * tk, :]
            carry, acc = _sb_tile(q, *kv_block(g), later_m, jnp.zeros((tk, 1), F32),
                                  jnp.zeros((tk, d), F32), causal)
            nxt = g - 1
            if not (sequence_start and s == 0):
                carry, acc = _sb_tile(q, *kv_block(g - 1), later_m, carry, acc, None)
                nxt = g - 2
            write(s, acc)
            state.append((q, nxt, carry, acc))

        worst = functools.reduce(jnp.maximum, [c for _, _, c, _ in state])

        @pl.when(jnp.max(worst) > SB_DEAD_LOG2_WEIGHT)
        def _():
            for s, (q, nxt, carry, acc) in enumerate(state):
                def cond(st):
                    j, alive, _, _ = st
                    return jnp.logical_and(j >= 0, alive)

                def body(st, q=q):
                    j, _, carry, acc = st
                    carry, acc = _sb_tile(q, *kv_block(j), later_m, carry, acc, None)
                    return j - 1, jnp.max(carry) > SB_DEAD_LOG2_WEIGHT, carry, acc

                alive = jnp.max(carry) > SB_DEAD_LOG2_WEIGHT
                _, _, _, acc = lax.while_loop(cond, body, (nxt, alive, carry, acc))
                write(s, acc)

    @pl.when(qi == 0)
    def _():
        step(True)

    @pl.when(qi > 0)
    def _():
        step(False)


def _stick_breaking(u, tq, tk):
    b, s, four_d = u.shape
    dm = four_d // 4
    d = SB_HEAD_DIM
    heads = dm // d
    row = jnp.arange(tk)
    later_m = (row[:, None] > row[None, :]).astype(BF16)
    blocks = 2 * _nbytes((s, d), BF16) + 3 * _nbytes((tq, d), BF16)
    resident = _nbytes(later_m.shape, BF16) + (tq // tk) * 12 * _nbytes((tk, tk), F32)
    return pl.pallas_call(
        functools.partial(_sb_kernel, n_sub=tq // tk, tk=tk),
        grid=(b, heads, s // tq),
        in_specs=[
            pl.BlockSpec((1, tq, d), lambda bi, hi, qi: (bi, qi, hi)),
            pl.BlockSpec((1, s, d), lambda bi, hi, qi: (bi, 0, heads + hi)),
            pl.BlockSpec((1, s, d), lambda bi, hi, qi: (bi, 0, 2 * heads + hi)),
            pl.BlockSpec((1, tq, d), lambda bi, hi, qi: (bi, qi, 3 * heads + hi)),
            pl.BlockSpec(later_m.shape, lambda bi, hi, qi: (0, 0)),
        ],
        out_specs=pl.BlockSpec((1, tq, d), lambda bi, hi, qi: (bi, qi, hi)),
        out_shape=jax.ShapeDtypeStruct((b, s, dm), BF16),
        compiler_params=pltpu.CompilerParams(
            dimension_semantics=("arbitrary", "arbitrary", "arbitrary"),
            vmem_limit_bytes=_vmem_limit(blocks, resident)),
        name="stick_breaking",
    )(u, u, u, u, later_m)


def _xattn_kernel(q_ref, k_ref, v_ref, o_ref, *, heads):
    dm = q_ref.shape[2]
    hd = dm // heads
    scale = 1.0 / math.sqrt(hd)
    cols = [slice(h * hd, (h + 1) * hd) for h in range(heads)]
    s = [lax.dot_general(q_ref[0, :, c], k_ref[0, :, c], (((1,), (1,)), ((), ())),
                         preferred_element_type=F32) * scale for c in cols]
    e = [jnp.exp(x - jnp.max(x, axis=-1, keepdims=True)) for x in s]
    p = [(x * (1.0 / jnp.sum(x, axis=-1, keepdims=True))).astype(BF16) for x in e]
    o = [jnp.dot(x, v_ref[0, :, c], preferred_element_type=F32) for x, c in zip(p, cols)]
    for x, c in zip(o, cols):
        o_ref[0, :, c] = x.astype(o_ref.dtype)


def _cross_attention_core(q, kv, t):
    b, s, dm = q.shape
    mlen = kv.shape[1]
    blocks = 2 * _nbytes((t, dm), BF16) + 2 * _nbytes((mlen, dm), BF16)
    resident = 6 * _nbytes((t, mlen), F32) + 2 * _nbytes((t, dm // XA_HEADS), F32)
    return pl.pallas_call(
        functools.partial(_xattn_kernel, heads=XA_HEADS),
        grid=(b, s // t),
        in_specs=[
            pl.BlockSpec((1, t, dm), lambda bi, si: (bi, si, 0)),
            pl.BlockSpec((1, mlen, dm), lambda bi, si: (bi, 0, 0)),
            pl.BlockSpec((1, mlen, dm), lambda bi, si: (bi, 0, 1)),
        ],
        out_specs=pl.BlockSpec((1, t, dm), lambda bi, si: (bi, si, 0)),
        out_shape=jax.ShapeDtypeStruct((b, s, dm), BF16),
        compiler_params=pltpu.CompilerParams(
            dimension_semantics=("arbitrary", "arbitrary"),
            vmem_limit_bytes=_vmem_limit(blocks, resident)),
        name="cross_attention",
    )(q, kv, kv)


MM_TM = 1024
MM_TN = 1024
MM_TN_BF16_OUT = 2048
RES_TM = 512
POOL_ROWS = 512
HGRN_ROWS = 512
HGRN_BLOCK = 256
HGRN_HEADS_PER_STEP = 4
SB_Q_ROWS = 1024
SB_K_ROWS = 256
XA_ROWS = 512


def kernel(x, mem, norm_mix_g, norm_xa_g, norm_mem_g, final_norm_g, w_in_even, pool_w,
           pool_scale, hgrn_lower_bounds, hgrn_norm_g, w_out_even, w_in_odd, w_out_odd,
           xa_wq, xa_wkv, xa_wo):
    b, s, dm = x.shape
    mlen = mem.shape[1]
    depth = norm_mix_g.shape[0]
    rows = b * s
    xs = x.reshape(rows, dm)
    mem2 = mem.reshape(b * mlen, dm)
    sb_col_scale = jnp.concatenate([jnp.full((dm,), LOG2E / math.sqrt(SB_HEAD_DIM), F32),
                                    jnp.ones((3 * dm,), F32)])
    h = None
    for l in range(depth):
        if l % 2 == 0:
            e = l // 2
            w_in = w_in_even[e].astype(BF16)
            if h is None:
                u = _norm_matmul(xs, norm_mix_g[l], w_in, F32, MM_TM, MM_TN)
            else:
                u = _matmul(h, w_in, F32, MM_TM, MM_TN)
            u = u.reshape(b, s, -1)
            y_a = _pool_mixer(u, pool_w[e].astype(BF16), pool_scale[e], POOL_ROWS)
            y_b = _hgrn2(u, hgrn_lower_bounds, hgrn_norm_g[e], e, HGRN_ROWS,
                         HGRN_HEADS_PER_STEP)
            mixed = [y_a.reshape(rows, -1), y_b.reshape(rows, -1)]
            w_out = w_out_even[e].astype(BF16)
        else:
            o = l // 2
            w_in = (w_in_odd[o] * sb_col_scale).astype(BF16)
            if h is None:
                u = _norm_matmul(xs, norm_mix_g[l], w_in, BF16, MM_TM, MM_TN_BF16_OUT)
            else:
                u = _matmul(h, w_in, BF16, MM_TM, MM_TN_BF16_OUT)
            mixed = [_stick_breaking(u.reshape(b, s, -1), SB_Q_ROWS, SB_K_ROWS).reshape(rows, dm)]
            w_out = w_out_odd[o].astype(BF16)
        xs, h_xa = _matmul_residual(mixed, w_out, xs, norm_xa_g[l], RES_TM, "emit")
        q = _matmul(h_xa, xa_wq[l].astype(BF16), BF16, MM_TM, MM_TN_BF16_OUT)
        kv = _norm_matmul(mem2, norm_mem_g[l], xa_wkv[l].astype(BF16), BF16,
                          b * mlen, MM_TN)
        att = _cross_attention_core(q.reshape(b, s, dm), kv.reshape(b, mlen, 2 * dm), XA_ROWS)
        att = [att.reshape(rows, dm)]
        if l == depth - 1:
            (xs,) = _matmul_residual(att, xa_wo[l].astype(BF16), xs, final_norm_g, RES_TM, "final")
        else:
            xs, h = _matmul_residual(att, xa_wo[l].astype(BF16), xs, norm_mix_g[l + 1],
                                     RES_TM, "emit")
    return xs.reshape(b, s, dm)
```

```python
import functools
import math

import jax
import jax.numpy as jnp
from jax import lax
from jax.experimental import pallas as pl
from jax.experimental.pallas import tpu as pltpu

F32 = jnp.float32
BF16 = jnp.bfloat16

EPS = 1e-6
LOG2E = math.log2(math.e)
POOL_WINDOWS = (2, 4, 8, 16)
POOL_HALO = 32
F32_SUBLANES = 8
HGRN_HEAD_DIM = 128
HGRN_CHUNK = 32
BF16_SUBLANES = 16
SB_HEAD_DIM = 128
XA_HEADS = 4

V7X_VMEM_BYTES = 64 * 1024 * 1024
VMEM_CAP_BYTES = V7X_VMEM_BYTES - 8 * 1024 * 1024
COMPILER_SCRATCH_BYTES = 8 * 1024 * 1024

SB_DEAD_LOG2_WEIGHT = -151.0
SB_MASKED_LOGIT = -1e30


def _vmem_limit(pipelined_bytes, resident_bytes=0):
    need = 2 * pipelined_bytes + resident_bytes + COMPILER_SCRATCH_BYTES
    return int(min(VMEM_CAP_BYTES, need))


def _nbytes(shape, dtype):
    return math.prod(shape) * jnp.dtype(dtype).itemsize


def _sigmoid(x):
    return 1.0 / (1.0 + jnp.exp(-x))


def _silu(x):
    return x * _sigmoid(x)


def _split_bf16(x):
    hi = x.astype(BF16)
    lo = (x - hi.astype(F32)).astype(BF16)
    return hi, lo


def _rms_scale(x):
    return x * lax.rsqrt(jnp.mean(x * x, axis=-1, keepdims=True) + EPS)


NORM_ROWS = 256


def _norm_matmul_kernel(x_ref, g_ref, w_ref, o_ref, h_ref):
    tm = x_ref.shape[0]

    @pl.when(pl.program_id(1) == 0)
    def _():
        g = g_ref[...]
        for r in range(0, tm, NORM_ROWS):
            rows = slice(r, r + NORM_ROWS)
            h = (_rms_scale(x_ref[rows, :]) * g).astype(h_ref.dtype)
            h_ref[rows, :] = h
            o_ref[rows, :] = jnp.dot(h, w_ref[...],
                                     preferred_element_type=F32).astype(o_ref.dtype)

    @pl.when(pl.program_id(1) > 0)
    def _():
        o_ref[...] = jnp.dot(h_ref[...], w_ref[...],
                             preferred_element_type=F32).astype(o_ref.dtype)


def _norm_matmul(x, g, w, out_dtype, tm, tn):
    m, k = x.shape
    n = w.shape[1]
    assert m % tm == 0 and n % tn == 0 and tm % NORM_ROWS == 0
    blocks = (_nbytes((tm, k), x.dtype) + _nbytes((k, tn), w.dtype)
              + _nbytes((tm, tn), out_dtype))
    scratch = _nbytes((tm, k), BF16) + _nbytes((tm, tn), F32)
    return pl.pallas_call(
        _norm_matmul_kernel,
        grid=(m // tm, n // tn),
        in_specs=[
            pl.BlockSpec((tm, k), lambda i, j: (i, 0)),
            pl.BlockSpec((1, k), lambda i, j: (0, 0)),
            pl.BlockSpec((k, tn), lambda i, j: (0, j)),
        ],
        out_specs=pl.BlockSpec((tm, tn), lambda i, j: (i, j)),
        out_shape=jax.ShapeDtypeStruct((m, n), out_dtype),
        scratch_shapes=[pltpu.VMEM((tm, k), BF16)],
        compiler_params=pltpu.CompilerParams(
            dimension_semantics=("arbitrary", "arbitrary"),
            vmem_limit_bytes=_vmem_limit(blocks, scratch)),
        name="norm_matmul",
    )(x, g.reshape(1, k), w)


def _matmul_kernel(a_ref, w_ref, o_ref):
    res = jnp.dot(a_ref[...], w_ref[...], preferred_element_type=F32).astype(o_ref.dtype)
    if len(o_ref.shape) == 2:
        o_ref[...] = res
    else:
        head_dim = o_ref.shape[2]
        for hh in range(o_ref.shape[0]):
            o_ref[hh] = res[:, hh * head_dim:(hh + 1) * head_dim]


def _matmul(a, w, out_dtype, tm, tn, head_dim=None):
    m, k = a.shape
    n = w.shape[1]
    assert m % tm == 0 and n % tn == 0
    blocks = (_nbytes((tm, k), a.dtype) + _nbytes((k, tn), w.dtype)
              + _nbytes((tm, tn), out_dtype))
    if head_dim is None:
        out_spec = pl.BlockSpec((tm, tn), lambda i, j: (i, j))
        out_shape = jax.ShapeDtypeStruct((m, n), out_dtype)
    else:
        out_spec = pl.BlockSpec((tn // head_dim, tm, head_dim), lambda i, j: (j, i, 0))
        out_shape = jax.ShapeDtypeStruct((n // head_dim, m, head_dim), out_dtype)
    return pl.pallas_call(
        _matmul_kernel,
        grid=(m // tm, n // tn),
        in_specs=[pl.BlockSpec((tm, k), lambda i, j: (i, 0)),
                  pl.BlockSpec((k, tn), lambda i, j: (0, j))],
        out_specs=out_spec,
        out_shape=out_shape,
        compiler_params=pltpu.CompilerParams(
            dimension_semantics=("arbitrary", "arbitrary"),
            vmem_limit_bytes=_vmem_limit(blocks, _nbytes((tm, tn), F32))),
        name="matmul",
    )(a, w)


def _matmul_residual_kernel(*refs, n_lhs, norm_mode):
    lhs_refs = refs[:n_lhs]
    w_refs = refs[n_lhs:2 * n_lhs]
    res_ref, g_ref = refs[2 * n_lhs], refs[2 * n_lhs + 1]
    out_refs = refs[2 * n_lhs + 2:]
    acc = res_ref[...]
    for a_ref, w_ref in zip(lhs_refs, w_refs):
        acc = acc + jnp.dot(a_ref[...], w_ref[...], preferred_element_type=F32)
    normed = _rms_scale(acc) * g_ref[...]
    if norm_mode == "final":
        out_refs[0][...] = normed
    else:
        out_refs[0][...] = acc
        out_refs[1][...] = normed.astype(out_refs[1].dtype)


def _matmul_residual(lhs_list, w, res, g, tm, norm_mode):
    assert norm_mode in ("emit", "final")
    m, n = res.shape
    kp = lhs_list[0].shape[1]
    assert all(a.shape == (m, kp) for a in lhs_list)
    assert w.shape == (kp * len(lhs_list), n)
    n_lhs = len(lhs_list)
    row_block = pl.BlockSpec((tm, n), lambda i: (i, 0))
    in_specs = [pl.BlockSpec((tm, kp), lambda i: (i, 0)) for _ in lhs_list]
    in_specs += [pl.BlockSpec((kp, n), functools.partial(lambda i, p: (p, 0), p=p))
                 for p in range(n_lhs)]
    in_specs += [row_block, pl.BlockSpec((1, n), lambda i: (0, 0))]
    out_shape = [jax.ShapeDtypeStruct((m, n), F32)]
    if norm_mode == "emit":
        out_shape += [jax.ShapeDtypeStruct((m, n), BF16)]
    blocks = (n_lhs * (_nbytes((tm, kp), BF16) + _nbytes((kp, n), BF16))
              + 2 * _nbytes((tm, n), F32) + _nbytes((tm, n), BF16))
    return pl.pallas_call(
        functools.partial(_matmul_residual_kernel, n_lhs=n_lhs, norm_mode=norm_mode),
        grid=(m // tm,),
        in_specs=in_specs,
        out_specs=[row_block] * len(out_shape),
        out_shape=out_shape,
        compiler_params=pltpu.CompilerParams(
            dimension_semantics=("arbitrary",),
            vmem_limit_bytes=_vmem_limit(blocks, 2 * _nbytes((tm, n), F32))),
        name="matmul_residual",
    )(*lhs_list, *([w] * n_lhs), res, g.reshape(1, n))


def _pool_kernel(a_ref, gate_ref, w_ref, scale_ref, o_ref, buf_ref, part_a, part_b, *, group):
    t = a_ref.shape[1]
    end = POOL_HALO + t
    sblk = pl.program_id(1)

    @pl.when(sblk == 0)
    def _():
        buf_ref[0:POOL_HALO, :] = jnp.zeros((POOL_HALO, buf_ref.shape[1]), F32)

    buf_ref[POOL_HALO:end, :] = a_ref[0]
    pos = sblk * t + lax.broadcasted_iota(jnp.int32, (t, 1), 0)
    for gi, win in enumerate(POOL_WINDOWS):
        cols = slice(gi * group, (gi + 1) * group)
        cur = buf_ref[POOL_HALO:end, cols]
        levels = win.bit_length() - 1
        load = lambda lo, hi, cols=cols: buf_ref[lo:hi, cols]
        for k in range(1, levels + 1):
            shift = 2 ** (k - 1)
            lo = POOL_HALO - F32_SUBLANES * (levels - k)
            tot = load(lo, end) + load(lo - shift, end - shift)
            if k < levels:
                part = part_a if k % 2 else part_b
                part[lo:end, :] = tot
                load = lambda lo, hi, part=part: part[lo:hi, :]
        cnt = jnp.minimum(pos + 1, win).astype(F32)
        mixed = tot / cnt - cur
        y = jnp.dot(mixed.astype(BF16), w_ref[gi], preferred_element_type=F32)
        y = y * scale_ref[:, cols]
        o_ref[0, :, cols] = (y * _silu(gate_ref[0, :, cols])).astype(o_ref.dtype)
    buf_ref[0:POOL_HALO, :] = buf_ref[t:t + POOL_HALO, :]


def _pool_mixer(u, pool_w, pool_scale, t):
    b, s, _ = u.shape
    n_groups, group, _ = pool_w.shape
    width = n_groups * group
    blocks = 2 * _nbytes((t, width), F32) + _nbytes((t, width), BF16)
    assert POOL_HALO >= F32_SUBLANES * (max(POOL_WINDOWS).bit_length() - 1)
    resident = (_nbytes((t + POOL_HALO, width), F32) + _nbytes(pool_w.shape, BF16)
                + 6 * _nbytes((t + POOL_HALO, group), F32))
    return pl.pallas_call(
        functools.partial(_pool_kernel, group=group),
        grid=(b, s // t),
        in_specs=[
            pl.BlockSpec((1, t, width), lambda bi, si: (bi, si, 0)),
            pl.BlockSpec((1, t, width), lambda bi, si: (bi, si, 1)),
            pl.BlockSpec(pool_w.shape, lambda bi, si: (0, 0, 0)),
            pl.BlockSpec((1, width), lambda bi, si: (0, 0)),
        ],
        out_specs=pl.BlockSpec((1, t, width), lambda bi, si: (bi, si, 0)),
        out_shape=jax.ShapeDtypeStruct((b, s, width), BF16),
        scratch_shapes=[pltpu.VMEM((t + POOL_HALO, width), F32),
                        pltpu.VMEM((t + POOL_HALO, group), F32),
                        pltpu.VMEM((t + POOL_HALO, group), F32)],
        compiler_params=pltpu.CompilerParams(
            dimension_semantics=("arbitrary", "arbitrary"),
            vmem_limit_bytes=_vmem_limit(blocks, resident)),
        name="pool_mixer",
    )(u, u, pool_w, pool_scale.reshape(1, width))


def _hgrn_kernel(q_ref, f_ref, i_ref, gate_ref, lbp_ref, ng_ref, cum_ref,
                 o_ref, state_ref, *, layer, heads_per_step):
    t = q_ref.shape[1]
    d = HGRN_HEAD_DIM
    gw = heads_per_step * d
    n_chunks = t // HGRN_CHUNK
    nt_dims = (((1,), (1,)), ((), ()))
    tn_dims = (((0,), (0,)), ((), ()))

    @pl.when(pl.program_id(2) == 0)
    def _():
        state_ref[...] = jnp.zeros(state_ref.shape, F32)

    lbp = lbp_ref[...]
    ex = jnp.exp(lbp - jnp.max(lbp, axis=0, keepdims=True))
    lb = jnp.sum(ex[:layer + 1], axis=0, keepdims=True) / jnp.sum(ex, axis=0, keepdims=True)

    f = lb + (1.0 - lb) * _sigmoid(f_ref[0])
    logf = jnp.log(f)
    k = 1.0 - f

    cum_m = cum_ref[...]
    blk = cum_m.shape[1]
    blk_chunks = blk // HGRN_CHUNK
    blocks = [slice(r, r + blk) for r in range(0, t, blk)]
    hi, lo = _split_bf16(logf)
    hilo = jnp.concatenate([hi, lo], axis=1)
    sums = [jnp.dot(cum_m, hilo[r], preferred_element_type=F32) for r in blocks]
    sums = [x[:, :gw] + x[:, gw:] for x in sums]
    bcum = jnp.concatenate([x[:blk] for x in sums], axis=0)
    decay = jnp.exp(jnp.concatenate([x[blk:blk + blk_chunks] for x in sums], axis=0))

    q_t = (q_ref[0] * jnp.exp(bcum)).astype(BF16)
    k_t = k * jnp.exp(-bcum)
    k_t_b = k_t.astype(BF16)
    inp_b = i_ref[0].astype(BF16)
    in_chunk = cum_m[:blk] > 0
    heads = [slice(h * d, (h + 1) * d) for h in range(heads_per_step)]
    chunks = [slice(n * HGRN_CHUNK, (n + 1) * HGRN_CHUNK) for n in range(n_chunks)]

    scores = [[lax.dot_general(q_t[r, c], k_t_b[r, c], nt_dims, preferred_element_type=F32)
               for r in blocks] for c in heads]
    scores = [[jnp.where(in_chunk, s, 0.0).astype(BF16) for s in per_head]
              for per_head in scores]
    o_intra = [jnp.concatenate([jnp.dot(s, inp_b[r, c], preferred_element_type=F32)
                                for s, r in zip(per_head, blocks)], axis=0)
               for per_head, c in zip(scores, heads)]

    k_end = [(k_t[r] * decay[n:n + 1, :]).astype(BF16) for n, r in enumerate(chunks)]
    incr = [[lax.dot_general(inp_b[r, c], k_end[n][:, c], tn_dims, preferred_element_type=F32)
             for c in heads] for n, r in enumerate(chunks)]
    states = []
    cur = [state_ref[h] for h in range(heads_per_step)]
    for n in range(n_chunks):
        states.append([s.astype(BF16) for s in cur])
        cur = [s * decay[n:n + 1, c] + u for s, c, u in zip(cur, heads, incr[n])]
    for h in range(heads_per_step):
        state_ref[h] = cur[h]

    o_inter = [jnp.concatenate(
        [lax.dot_general(q_t[r, c], states[n][h], nt_dims, preferred_element_type=F32)
         for n, r in enumerate(chunks)], axis=0) for h, c in enumerate(heads)]

    for h, c in enumerate(heads):
        o = _rms_scale(o_intra[h] + o_inter[h]) * ng_ref[:, c]
        o_ref[0, :, c] = (o * _silu(gate_ref[0, :, c])).astype(o_ref.dtype)


def _hgrn2(u, lower_bounds, norm_g, layer, t, heads_per_step):
    b, s, six_w = u.shape
    width = six_w // 6
    d = HGRN_HEAD_DIM
    gw = heads_per_step * d
    groups = width // gw
    blk = HGRN_BLOCK
    assert t % blk == 0 and blk // HGRN_CHUNK <= BF16_SUBLANES
    pos = jnp.arange(blk, dtype=jnp.int32)
    chunk_id = pos // HGRN_CHUNK
    tril = (chunk_id[:, None] == chunk_id[None, :]) & (pos[None, :] <= pos[:, None])
    select = jnp.arange(BF16_SUBLANES, dtype=jnp.int32)[:, None] == chunk_id[None, :]
    cum_m = jnp.concatenate([tril, select], axis=0).astype(BF16)
    col = lambda base: (lambda bi, gi, si: (bi, si, base * groups + gi))
    blocks = 4 * _nbytes((t, gw), F32) + _nbytes((t, gw), BF16)
    resident = (2 * _nbytes(cum_m.shape, BF16) + _nbytes((heads_per_step, d, d), F32)
                + heads_per_step * (16 * _nbytes((t, d), F32) + 2 * _nbytes((t, blk), F32)))
    return pl.pallas_call(
        functools.partial(_hgrn_kernel, layer=layer, heads_per_step=heads_per_step),
        grid=(b, groups, s // t),
        in_specs=[
            pl.BlockSpec((1, t, gw), col(2)),
            pl.BlockSpec((1, t, gw), col(3)),
            pl.BlockSpec((1, t, gw), col(4)),
            pl.BlockSpec((1, t, gw), col(5)),
            pl.BlockSpec((lower_bounds.shape[0], gw), lambda bi, gi, si: (0, gi)),
            pl.BlockSpec((1, gw), lambda bi, gi, si: (0, gi)),
            pl.BlockSpec(cum_m.shape, lambda bi, gi, si: (0, 0)),
        ],
        out_specs=pl.BlockSpec((1, t, gw), lambda bi, gi, si: (bi, si, gi)),
        out_shape=jax.ShapeDtypeStruct((b, s, width), BF16),
        scratch_shapes=[pltpu.VMEM((heads_per_step, d, d), F32)],
        compiler_params=pltpu.CompilerParams(
            dimension_semantics=("arbitrary", "arbitrary", "arbitrary"),
            vmem_limit_bytes=_vmem_limit(blocks, resident)),
        name="hgrn2",
    )(u, u, u, u, lower_bounds, norm_g.reshape(1, width), cum_m)


def _sb_tile(q, kb, vb, later_m, carry, acc, mask):
    z = lax.dot_general(q, kb, (((1,), (1,)), ((), ())), preferred_element_type=F32)
    if mask is not None:
        z = jnp.where(mask, z, SB_MASKED_LOGIT)
    soft = jnp.log(1.0 + jnp.exp2(-jnp.abs(z))) * LOG2E
    log_beta = jnp.minimum(z, 0.0) - soft
    log_keep = log_beta - z
    later = jnp.dot(log_keep.astype(BF16), later_m, preferred_element_type=F32)
    a = jnp.exp2(log_beta + later + carry)
    acc = acc + jnp.dot(a.astype(BF16), vb, preferred_element_type=F32)
    return carry + jnp.sum(log_keep, axis=-1, keepdims=True), acc


def _sb_kernel(q_ref, k_ref, v_ref, gate_ref, m_ref, o_ref, *, n_sub, tk):
    d = q_ref.shape[2]
    qi = pl.program_id(2)
    later_m = m_ref[...]
    causal = (lax.broadcasted_iota(jnp.int32, (tk, tk), 1)
              < lax.broadcasted_iota(jnp.int32, (tk, tk), 0))

    def kv_block(j):
        rows = pl.ds(pl.multiple_of(j * tk, tk), tk)
        return k_ref[0, rows, :], v_ref[0, rows, :]

    def write(s, acc):
        rows = slice(s * tk, (s + 1) * tk)
        o_ref[0, rows, :] = (acc * _silu(gate_ref[0, rows, :].astype(F32))).astype(o_ref.dtype)

    def step(sequence_start):
        state = []
        for s in range(n_sub):
            g = qi * n_sub + s
            q = q_ref[0, s * tk:(s + 1) * tk, :]
            carry, acc = _sb_tile(q, *kv_block(g), later_m, jnp.zeros((tk, 1), F32),
                                  jnp.zeros((tk, d), F32), causal)
            nxt = g - 1
            if not (sequence_start and s == 0):
                carry, acc = _sb_tile(q, *kv_block(g - 1), later_m, carry, acc, None)
                nxt = g - 2
            write(s, acc)
            state.append((q, nxt, carry, acc))

        worst = functools.reduce(jnp.maximum, [c for _, _, c, _ in state])

        @pl.when(jnp.max(worst) > SB_DEAD_LOG2_WEIGHT)
        def _():
            for s, (q, nxt, carry, acc) in enumerate(state):
                def cond(st):
                    j, alive, _, _ = st
                    return jnp.logical_and(j >= 0, alive)

                def body(st, q=q):
                    j, _, carry, acc = st
                    carry, acc = _sb_tile(q, *kv_block(j), later_m, carry, acc, None)
                    return j - 1, jnp.max(carry) > SB_DEAD_LOG2_WEIGHT, carry, acc

                alive = jnp.max(carry) > SB_DEAD_LOG2_WEIGHT
                _, _, _, acc = lax.while_loop(cond, body, (nxt, alive, carry, acc))
                write(s, acc)

    @pl.when(qi == 0)
    def _():
        step(True)

    @pl.when(qi > 0)
    def _():
        step(False)


def _stick_breaking(u, b, tq, tk):
    four_h, rows, d = u.shape
    heads = four_h // 4
    dm = heads * d
    s = rows // b
    nq = s // tq
    row = jnp.arange(tk)
    later_m = (row[:, None] > row[None, :]).astype(BF16)
    blocks = 2 * _nbytes((s, d), BF16) + 3 * _nbytes((tq, d), BF16)
    resident = _nbytes(later_m.shape, BF16) + (tq // tk) * 12 * _nbytes((tk, tk), F32)
    return pl.pallas_call(
        functools.partial(_sb_kernel, n_sub=tq // tk, tk=tk),
        grid=(b, heads, nq),
        in_specs=[
            pl.BlockSpec((1, tq, d), lambda bi, hi, qi: (hi, bi * nq + qi, 0)),
            pl.BlockSpec((1, s, d), lambda bi, hi, qi: (heads + hi, bi, 0)),
            pl.BlockSpec((1, s, d), lambda bi, hi, qi: (2 * heads + hi, bi, 0)),
            pl.BlockSpec((1, tq, d), lambda bi, hi, qi: (3 * heads + hi, bi * nq + qi, 0)),
            pl.BlockSpec(later_m.shape, lambda bi, hi, qi: (0, 0)),
        ],
        out_specs=pl.BlockSpec((1, tq, d), lambda bi, hi, qi: (bi, qi, hi)),
        out_shape=jax.ShapeDtypeStruct((b, s, dm), BF16),
        compiler_params=pltpu.CompilerParams(
            dimension_semantics=("arbitrary", "arbitrary", "arbitrary"),
            vmem_limit_bytes=_vmem_limit(blocks, resident)),
        name="stick_breaking",
    )(u, u, u, u, later_m)


def _xattn_kernel(q_ref, k_ref, v_ref, o_ref, *, heads):
    dm = q_ref.shape[2]
    hd = dm // heads
    scale = 1.0 / math.sqrt(hd)
    cols = [slice(h * hd, (h + 1) * hd) for h in range(heads)]
    s = [lax.dot_general(q_ref[0, :, c], k_ref[0, :, c], (((1,), (1,)), ((), ())),
                         preferred_element_type=F32) * scale for c in cols]
    e = [jnp.exp(x - jnp.max(x, axis=-1, keepdims=True)) for x in s]
    p = [(x * (1.0 / jnp.sum(x, axis=-1, keepdims=True))).astype(BF16) for x in e]
    o = [jnp.dot(x, v_ref[0, :, c], preferred_element_type=F32) for x, c in zip(p, cols)]
    for x, c in zip(o, cols):
        o_ref[0, :, c] = x.astype(o_ref.dtype)


def _cross_attention_core(q, kv, t):
    b, s, dm = q.shape
    mlen = kv.shape[1]
    blocks = 2 * _nbytes((t, dm), BF16) + 2 * _nbytes((mlen, dm), BF16)
    resident = 6 * _nbytes((t, mlen), F32) + 2 * _nbytes((t, dm // XA_HEADS), F32)
    return pl.pallas_call(
        functools.partial(_xattn_kernel, heads=XA_HEADS),
        grid=(b, s // t),
        in_specs=[
            pl.BlockSpec((1, t, dm), lambda bi, si: (bi, si, 0)),
            pl.BlockSpec((1, mlen, dm), lambda bi, si: (bi, 0, 0)),
            pl.BlockSpec((1, mlen, dm), lambda bi, si: (bi, 0, 1)),
        ],
        out_specs=pl.BlockSpec((1, t, dm), lambda bi, si: (bi, si, 0)),
        out_shape=jax.ShapeDtypeStruct((b, s, dm), BF16),
        compiler_params=pltpu.CompilerParams(
            dimension_semantics=("arbitrary", "arbitrary"),
            vmem_limit_bytes=_vmem_limit(blocks, resident)),
        name="cross_attention",
    )(q, kv, kv)


MM_TM = 1024
MM_TN = 1024
MM_TN_BF16_OUT = 2048
RES_TM = 512
POOL_ROWS = 512
HGRN_ROWS = 512
HGRN_BLOCK = 256
HGRN_HEADS_PER_STEP = 4
SB_Q_ROWS = 1024
SB_K_ROWS = 256
XA_ROWS = 512


def kernel(x, mem, norm_mix_g, norm_xa_g, norm_mem_g, final_norm_g, w_in_even, pool_w,
           pool_scale, hgrn_lower_bounds, hgrn_norm_g, w_out_even, w_in_odd, w_out_odd,
           xa_wq, xa_wkv, xa_wo):
    b, s, dm = x.shape
    mlen = mem.shape[1]
    depth = norm_mix_g.shape[0]
    rows = b * s
    xs = x.reshape(rows, dm)
    mem2 = mem.reshape(b * mlen, dm)
    sb_col_scale = jnp.concatenate([jnp.full((dm,), LOG2E / math.sqrt(SB_HEAD_DIM), F32),
                                    jnp.ones((3 * dm,), F32)])
    h = None
    for l in range(depth):
        if l % 2 == 0:
            e = l // 2
            w_in = w_in_even[e].astype(BF16)
            if h is None:
                u = _norm_matmul(xs, norm_mix_g[l], w_in, F32, MM_TM, MM_TN)
            else:
                u = _matmul(h, w_in, F32, MM_TM, MM_TN)
            u = u.reshape(b, s, -1)
            y_a = _pool_mixer(u, pool_w[e].astype(BF16), pool_scale[e], POOL_ROWS)
            y_b = _hgrn2(u, hgrn_lower_bounds, hgrn_norm_g[e], e, HGRN_ROWS,
                         HGRN_HEADS_PER_STEP)
            mixed = [y_a.reshape(rows, -1), y_b.reshape(rows, -1)]
            w_out = w_out_even[e].astype(BF16)
        else:
            o = l // 2
            w_in = (w_in_odd[o] * sb_col_scale).astype(BF16)
            assert h is not None, "stick-breaking layers follow a layer that emits h"
            u = _matmul(h, w_in, BF16, MM_TM, MM_TN_BF16_OUT, head_dim=SB_HEAD_DIM)
            mixed = [_stick_breaking(u, b, SB_Q_ROWS, SB_K_ROWS).reshape(rows, dm)]
            w_out = w_out_odd[o].astype(BF16)
        xs, h_xa = _matmul_residual(mixed, w_out, xs, norm_xa_g[l], RES_TM, "emit")
        q = _matmul(h_xa, xa_wq[l].astype(BF16), BF16, MM_TM, MM_TN_BF16_OUT)
        kv = _norm_matmul(mem2, norm_mem_g[l], xa_wkv[l].astype(BF16), BF16,
                          b * mlen, MM_TN)
        att = _cross_attention_core(q.reshape(b, s, dm), kv.reshape(b, mlen, 2 * dm), XA_ROWS)
        att = [att.reshape(rows, dm)]
        if l == depth - 1:
            (xs,) = _matmul_residual(att, xa_wo[l].astype(BF16), xs, final_norm_g, RES_TM, "final")
        else:
            xs, h = _matmul_residual(att, xa_wo[l].astype(BF16), xs, norm_mix_g[l + 1],
                                     RES_TM, "emit")
    return xs.reshape(b, s, dm)
```

```python
import functools
import math

import jax
import jax.numpy as jnp
from jax import lax
from jax.experimental import pallas as pl
from jax.experimental.pallas import tpu as pltpu

F32 = jnp.float32
BF16 = jnp.bfloat16

EPS = 1e-6
LOG2E = math.log2(math.e)
POOL_WINDOWS = (2, 4, 8, 16)
POOL_HALO = 32
F32_SUBLANES = 8
HGRN_HEAD_DIM = 128
HGRN_CHUNK = 32
BF16_SUBLANES = 16
SB_HEAD_DIM = 128
XA_HEADS = 4

V7X_VMEM_BYTES = 64 * 1024 * 1024
VMEM_CAP_BYTES = V7X_VMEM_BYTES - 8 * 1024 * 1024
COMPILER_SCRATCH_BYTES = 8 * 1024 * 1024

SB_DEAD_LOG2_WEIGHT = -151.0
SB_MASKED_LOGIT = -1e30


def _vmem_limit(pipelined_bytes, resident_bytes=0):
    need = 2 * pipelined_bytes + resident_bytes + COMPILER_SCRATCH_BYTES
    return int(min(VMEM_CAP_BYTES, need))


def _nbytes(shape, dtype):
    return math.prod(shape) * jnp.dtype(dtype).itemsize


def _sigmoid(x):
    return 1.0 / (1.0 + jnp.exp(-x))


def _silu(x):
    return x * _sigmoid(x)


def _split_bf16(x):
    hi = x.astype(BF16)
    lo = (x - hi.astype(F32)).astype(BF16)
    return hi, lo


def _rms_scale(x):
    return x * lax.rsqrt(jnp.mean(x * x, axis=-1, keepdims=True) + EPS)


NORM_ROWS = 256


def _norm_matmul_kernel(x_ref, g_ref, w_ref, o_ref, h_ref):
    tm = x_ref.shape[0]

    @pl.when(pl.program_id(1) == 0)
    def _():
        g = g_ref[...]
        for r in range(0, tm, NORM_ROWS):
            rows = slice(r, r + NORM_ROWS)
            h = (_rms_scale(x_ref[rows, :]) * g).astype(h_ref.dtype)
            h_ref[rows, :] = h
            o_ref[rows, :] = jnp.dot(h, w_ref[...],
                                     preferred_element_type=F32).astype(o_ref.dtype)

    @pl.when(pl.program_id(1) > 0)
    def _():
        o_ref[...] = jnp.dot(h_ref[...], w_ref[...],
                             preferred_element_type=F32).astype(o_ref.dtype)


def _norm_matmul(x, g, w, layer, out_dtype, tm, tn):
    m, k = x.shape
    n = w.shape[2]
    assert m % tm == 0 and n % tn == 0 and tm % NORM_ROWS == 0
    blocks = (_nbytes((tm, k), x.dtype) + _nbytes((k, tn), w.dtype)
              + _nbytes((tm, tn), out_dtype))
    scratch = _nbytes((tm, k), BF16) + _nbytes((tm, tn), F32)
    return pl.pallas_call(
        _norm_matmul_kernel,
        grid=(m // tm, n // tn),
        in_specs=[
            pl.BlockSpec((tm, k), lambda i, j: (i, 0)),
            pl.BlockSpec((1, k), lambda i, j: (0, 0)),
            pl.BlockSpec((None, k, tn), lambda i, j: (layer, 0, j)),
        ],
        out_specs=pl.BlockSpec((tm, tn), lambda i, j: (i, j)),
        out_shape=jax.ShapeDtypeStruct((m, n), out_dtype),
        scratch_shapes=[pltpu.VMEM((tm, k), BF16)],
        compiler_params=pltpu.CompilerParams(
            dimension_semantics=("arbitrary", "arbitrary"),
            vmem_limit_bytes=_vmem_limit(blocks, scratch)),
        name="norm_matmul",
    )(x, g.reshape(1, k), w)


def _matmul_kernel(a_ref, w_ref, o_ref):
    res = jnp.dot(a_ref[...], w_ref[...], preferred_element_type=F32).astype(o_ref.dtype)
    if len(o_ref.shape) == 2:
        o_ref[...] = res
    else:
        head_dim = o_ref.shape[2]
        for hh in range(o_ref.shape[0]):
            o_ref[hh] = res[:, hh * head_dim:(hh + 1) * head_dim]


def _matmul(a, w, layer, out_dtype, tm, tn, head_dim=None):
    m, k = a.shape
    n = w.shape[2]
    assert m % tm == 0 and n % tn == 0
    blocks = (_nbytes((tm, k), a.dtype) + _nbytes((k, tn), w.dtype)
              + _nbytes((tm, tn), out_dtype))
    if head_dim is None:
        out_spec = pl.BlockSpec((tm, tn), lambda i, j: (i, j))
        out_shape = jax.ShapeDtypeStruct((m, n), out_dtype)
    else:
        out_spec = pl.BlockSpec((tn // head_dim, tm, head_dim), lambda i, j: (j, i, 0))
        out_shape = jax.ShapeDtypeStruct((n // head_dim, m, head_dim), out_dtype)
    return pl.pallas_call(
        _matmul_kernel,
        grid=(m // tm, n // tn),
        in_specs=[pl.BlockSpec((tm, k), lambda i, j: (i, 0)),
                  pl.BlockSpec((None, k, tn), lambda i, j: (layer, 0, j))],
        out_specs=out_spec,
        out_shape=out_shape,
        compiler_params=pltpu.CompilerParams(
            dimension_semantics=("arbitrary", "arbitrary"),
            vmem_limit_bytes=_vmem_limit(blocks, _nbytes((tm, tn), F32))),
        name="matmul",
    )(a, w)


def _matmul_residual_kernel(*refs, n_lhs, norm_mode):
    lhs_refs = refs[:n_lhs]
    w_refs = refs[n_lhs:2 * n_lhs]
    res_ref, g_ref = refs[2 * n_lhs], refs[2 * n_lhs + 1]
    out_refs = refs[2 * n_lhs + 2:]
    acc = res_ref[...]
    for a_ref, w_ref in zip(lhs_refs, w_refs):
        acc = acc + jnp.dot(a_ref[...], w_ref[...], preferred_element_type=F32)
    normed = _rms_scale(acc) * g_ref[...]
    if norm_mode == "final":
        out_refs[0][...] = normed
    else:
        out_refs[0][...] = acc
        out_refs[1][...] = normed.astype(out_refs[1].dtype)


def _matmul_residual(lhs_list, w, layer, res, g, tm, norm_mode):
    assert norm_mode in ("emit", "final")
    m, n = res.shape
    kp = lhs_list[0].shape[1]
    assert all(a.shape == (m, kp) for a in lhs_list)
    assert w.shape[1:] == (kp * len(lhs_list), n)
    n_lhs = len(lhs_list)
    row_block = pl.BlockSpec((tm, n), lambda i: (i, 0))
    in_specs = [pl.BlockSpec((tm, kp), lambda i: (i, 0)) for _ in lhs_list]
    in_specs += [pl.BlockSpec((None, kp, n), functools.partial(lambda i, p: (layer, p, 0), p=p))
                 for p in range(n_lhs)]
    in_specs += [row_block, pl.BlockSpec((1, n), lambda i: (0, 0))]
    out_shape = [jax.ShapeDtypeStruct((m, n), F32)]
    if norm_mode == "emit":
        out_shape += [jax.ShapeDtypeStruct((m, n), BF16)]
    blocks = (n_lhs * (_nbytes((tm, kp), BF16) + _nbytes((kp, n), BF16))
              + 2 * _nbytes((tm, n), F32) + _nbytes((tm, n), BF16))
    return pl.pallas_call(
        functools.partial(_matmul_residual_kernel, n_lhs=n_lhs, norm_mode=norm_mode),
        grid=(m // tm,),
        in_specs=in_specs,
        out_specs=[row_block] * len(out_shape),
        out_shape=out_shape,
        compiler_params=pltpu.CompilerParams(
            dimension_semantics=("arbitrary",),
            vmem_limit_bytes=_vmem_limit(blocks, 2 * _nbytes((tm, n), F32))),
        name="matmul_residual",
    )(*lhs_list, *([w] * n_lhs), res, g.reshape(1, n))


def _pool_kernel(a_ref, gate_ref, w_ref, scale_ref, o_ref, buf_ref, part_a, part_b, *, group):
    t = a_ref.shape[1]
    end = POOL_HALO + t
    sblk = pl.program_id(1)

    @pl.when(sblk == 0)
    def _():
        buf_ref[0:POOL_HALO, :] = jnp.zeros((POOL_HALO, buf_ref.shape[1]), F32)

    buf_ref[POOL_HALO:end, :] = a_ref[0]
    pos = sblk * t + lax.broadcasted_iota(jnp.int32, (t, 1), 0)
    for gi, win in enumerate(POOL_WINDOWS):
        cols = slice(gi * group, (gi + 1) * group)
        cur = buf_ref[POOL_HALO:end, cols]
        levels = win.bit_length() - 1
        load = lambda lo, hi, cols=cols: buf_ref[lo:hi, cols]
        for k in range(1, levels + 1):
            shift = 2 ** (k - 1)
            lo = POOL_HALO - F32_SUBLANES * (levels - k)
            tot = load(lo, end) + load(lo - shift, end - shift)
            if k < levels:
                part = part_a if k % 2 else part_b
                part[lo:end, :] = tot
                load = lambda lo, hi, part=part: part[lo:hi, :]
        cnt = jnp.minimum(pos + 1, win).astype(F32)
        mixed = tot / cnt - cur
        y = jnp.dot(mixed.astype(BF16), w_ref[gi], preferred_element_type=F32)
        y = y * scale_ref[:, cols]
        o_ref[0, :, cols] = (y * _silu(gate_ref[0, :, cols])).astype(o_ref.dtype)
    buf_ref[0:POOL_HALO, :] = buf_ref[t:t + POOL_HALO, :]


def _pool_mixer(u, pool_w, pool_scale, t):
    b, s, _ = u.shape
    n_groups, group, _ = pool_w.shape
    width = n_groups * group
    blocks = 2 * _nbytes((t, width), F32) + _nbytes((t, width), BF16)
    assert POOL_HALO >= F32_SUBLANES * (max(POOL_WINDOWS).bit_length() - 1)
    resident = (_nbytes((t + POOL_HALO, width), F32) + _nbytes(pool_w.shape, BF16)
                + 6 * _nbytes((t + POOL_HALO, group), F32))
    return pl.pallas_call(
        functools.partial(_pool_kernel, group=group),
        grid=(b, s // t),
        in_specs=[
            pl.BlockSpec((1, t, width), lambda bi, si: (bi, si, 0)),
            pl.BlockSpec((1, t, width), lambda bi, si: (bi, si, 1)),
            pl.BlockSpec(pool_w.shape, lambda bi, si: (0, 0, 0)),
            pl.BlockSpec((1, width), lambda bi, si: (0, 0)),
        ],
        out_specs=pl.BlockSpec((1, t, width), lambda bi, si: (bi, si, 0)),
        out_shape=jax.ShapeDtypeStruct((b, s, width), BF16),
        scratch_shapes=[pltpu.VMEM((t + POOL_HALO, width), F32),
                        pltpu.VMEM((t + POOL_HALO, group), F32),
                        pltpu.VMEM((t + POOL_HALO, group), F32)],
        compiler_params=pltpu.CompilerParams(
            dimension_semantics=("arbitrary", "arbitrary"),
            vmem_limit_bytes=_vmem_limit(blocks, resident)),
        name="pool_mixer",
    )(u, u, pool_w, pool_scale.reshape(1, width))


def _hgrn_kernel(q_ref, f_ref, i_ref, gate_ref, lbp_ref, ng_ref, cum_ref,
                 o_ref, state_ref, *, layer, heads_per_step):
    t = q_ref.shape[1]
    d = HGRN_HEAD_DIM
    gw = heads_per_step * d
    n_chunks = t // HGRN_CHUNK
    nt_dims = (((1,), (1,)), ((), ()))
    tn_dims = (((0,), (0,)), ((), ()))

    @pl.when(pl.program_id(2) == 0)
    def _():
        state_ref[...] = jnp.zeros(state_ref.shape, F32)

    lbp = lbp_ref[...]
    ex = jnp.exp(lbp - jnp.max(lbp, axis=0, keepdims=True))
    lb = jnp.sum(ex[:layer + 1], axis=0, keepdims=True) / jnp.sum(ex, axis=0, keepdims=True)

    f = lb + (1.0 - lb) * _sigmoid(f_ref[0])
    logf = jnp.log(f)
    k = 1.0 - f

    cum_m = cum_ref[...]
    blk = cum_m.shape[1]
    blk_chunks = blk // HGRN_CHUNK
    blocks = [slice(r, r + blk) for r in range(0, t, blk)]
    hi, lo = _split_bf16(logf)
    hilo = jnp.concatenate([hi, lo], axis=1)
    sums = [jnp.dot(cum_m, hilo[r], preferred_element_type=F32) for r in blocks]
    sums = [x[:, :gw] + x[:, gw:] for x in sums]
    bcum = jnp.concatenate([x[:blk] for x in sums], axis=0)
    decay = jnp.exp(jnp.concatenate([x[blk:blk + blk_chunks] for x in sums], axis=0))

    q_t = (q_ref[0] * jnp.exp(bcum)).astype(BF16)
    k_t = k * jnp.exp(-bcum)
    k_t_b = k_t.astype(BF16)
    inp_b = i_ref[0].astype(BF16)
    in_chunk = cum_m[:blk] > 0
    heads = [slice(h * d, (h + 1) * d) for h in range(heads_per_step)]
    chunks = [slice(n * HGRN_CHUNK, (n + 1) * HGRN_CHUNK) for n in range(n_chunks)]

    scores = [[lax.dot_general(q_t[r, c], k_t_b[r, c], nt_dims, preferred_element_type=F32)
               for r in blocks] for c in heads]
    scores = [[jnp.where(in_chunk, s, 0.0).astype(BF16) for s in per_head]
              for per_head in scores]
    o_intra = [jnp.concatenate([jnp.dot(s, inp_b[r, c], preferred_element_type=F32)
                                for s, r in zip(per_head, blocks)], axis=0)
               for per_head, c in zip(scores, heads)]

    k_end = [(k_t[r] * decay[n:n + 1, :]).astype(BF16) for n, r in enumerate(chunks)]
    incr = [[lax.dot_general(inp_b[r, c], k_end[n][:, c], tn_dims, preferred_element_type=F32)
             for c in heads] for n, r in enumerate(chunks)]
    states = []
    cur = [state_ref[h] for h in range(heads_per_step)]
    for n in range(n_chunks):
        states.append([s.astype(BF16) for s in cur])
        cur = [s * decay[n:n + 1, c] + u for s, c, u in zip(cur, heads, incr[n])]
    for h in range(heads_per_step):
        state_ref[h] = cur[h]

    o_inter = [jnp.concatenate(
        [lax.dot_general(q_t[r, c], states[n][h], nt_dims, preferred_element_type=F32)
         for n, r in enumerate(chunks)], axis=0) for h, c in enumerate(heads)]

    for h, c in enumerate(heads):
        o = _rms_scale(o_intra[h] + o_inter[h]) * ng_ref[:, c]
        o_ref[0, :, c] = (o * _silu(gate_ref[0, :, c])).astype(o_ref.dtype)


def _hgrn2(u, lower_bounds, norm_g, layer, t, heads_per_step):
    b, s, six_w = u.shape
    width = six_w // 6
    d = HGRN_HEAD_DIM
    gw = heads_per_step * d
    groups = width // gw
    blk = HGRN_BLOCK
    assert t % blk == 0 and blk // HGRN_CHUNK <= BF16_SUBLANES
    pos = jnp.arange(blk, dtype=jnp.int32)
    chunk_id = pos // HGRN_CHUNK
    tril = (chunk_id[:, None] == chunk_id[None, :]) & (pos[None, :] <= pos[:, None])
    select = jnp.arange(BF16_SUBLANES, dtype=jnp.int32)[:, None] == chunk_id[None, :]
    cum_m = jnp.concatenate([tril, select], axis=0).astype(BF16)
    col = lambda base: (lambda bi, gi, si: (bi, si, base * groups + gi))
    blocks = 4 * _nbytes((t, gw), F32) + _nbytes((t, gw), BF16)
    resident = (2 * _nbytes(cum_m.shape, BF16) + _nbytes((heads_per_step, d, d), F32)
                + heads_per_step * (16 * _nbytes((t, d), F32) + 2 * _nbytes((t, blk), F32)))
    return pl.pallas_call(
        functools.partial(_hgrn_kernel, layer=layer, heads_per_step=heads_per_step),
        grid=(b, groups, s // t),
        in_specs=[
            pl.BlockSpec((1, t, gw), col(2)),
            pl.BlockSpec((1, t, gw), col(3)),
            pl.BlockSpec((1, t, gw), col(4)),
            pl.BlockSpec((1, t, gw), col(5)),
            pl.BlockSpec((lower_bounds.shape[0], gw), lambda bi, gi, si: (0, gi)),
            pl.BlockSpec((1, gw), lambda bi, gi, si: (0, gi)),
            pl.BlockSpec(cum_m.shape, lambda bi, gi, si: (0, 0)),
        ],
        out_specs=pl.BlockSpec((1, t, gw), lambda bi, gi, si: (bi, si, gi)),
        out_shape=jax.ShapeDtypeStruct((b, s, width), BF16),
        scratch_shapes=[pltpu.VMEM((heads_per_step, d, d), F32)],
        compiler_params=pltpu.CompilerParams(
            dimension_semantics=("arbitrary", "arbitrary", "arbitrary"),
            vmem_limit_bytes=_vmem_limit(blocks, resident)),
        name="hgrn2",
    )(u, u, u, u, lower_bounds, norm_g.reshape(1, width), cum_m)


def _sb_tiles(tiles, later_m):
    nt_dims = (((1,), (1,)), ((), ()))
    z = [lax.dot_general(q, kb, nt_dims, preferred_element_type=F32)
         for q, kb, _, _, _, _ in tiles]
    z = [x if t[5] is None else jnp.where(t[5], x, SB_MASKED_LOGIT) for x, t in zip(z, tiles)]
    soft = [jnp.log(1.0 + jnp.exp2(-jnp.abs(x))) * LOG2E for x in z]
    log_beta = [jnp.minimum(x, 0.0) - s for x, s in zip(z, soft)]
    log_keep = [lb - x for lb, x in zip(log_beta, z)]
    later = [jnp.dot(lk.astype(BF16), later_m, preferred_element_type=F32) for lk in log_keep]
    a = [jnp.exp2(lb + lt + t[3]).astype(BF16) for lb, lt, t in zip(log_beta, later, tiles)]
    acc = [t[4] + jnp.dot(x, t[2], preferred_element_type=F32) for x, t in zip(a, tiles)]
    carry = [t[3] + jnp.sum(lk, axis=-1, keepdims=True) for lk, t in zip(log_keep, tiles)]
    return list(zip(carry, acc))


def _sb_kernel(q_ref, k_ref, v_ref, gate_ref, m_ref, o_ref, *, n_sub, tr):
    d = q_ref.shape[2]
    qi = pl.program_id(2)
    later_wide = m_ref[...]
    later_m = later_wide[:tr, :tr]
    col = lax.broadcasted_iota(jnp.int32, (tr, 2 * tr), 1)
    row = lax.broadcasted_iota(jnp.int32, (tr, 2 * tr), 0)
    causal_window = col - tr < row
    causal_start = col < row

    def kv_blocks(j, n):
        rows = pl.ds(pl.multiple_of(j * tr, tr), n * tr)
        return k_ref[0, rows, :], v_ref[0, rows, :]

    def write(s, acc):
        rows = slice(s * tr, (s + 1) * tr)
        o_ref[0, rows, :] = (acc * _silu(gate_ref[0, rows, :].astype(F32))).astype(o_ref.dtype)

    def step(sequence_start):
        qs = [q_ref[0, s * tr:(s + 1) * tr, :] for s in range(n_sub)]
        blk = [qi * n_sub + s for s in range(n_sub)]
        zero = (jnp.zeros((tr, 1), F32), jnp.zeros((tr, d), F32))
        first = []
        for s in range(n_sub):
            if sequence_start and s == 0:
                first.append((qs[s], *kv_blocks(0, 2), *zero, causal_start))
            else:
                first.append((qs[s], *kv_blocks(blk[s] - 1, 2), *zero, causal_window))
        done = _sb_tiles(first, later_wide)
        nxt = [-1 if (sequence_start and s == 0) else blk[s] - 2 for s in range(n_sub)]
        has_second = [not (sequence_start and s < 2) for s in range(n_sub)]
        second = [(qs[s], *kv_blocks(blk[s] - 2, 1), *done[s], None)
                  for s in range(n_sub) if has_second[s]]
        for s, res in zip([s for s in range(n_sub) if has_second[s]], _sb_tiles(second, later_m)):
            done[s] = res
            nxt[s] = blk[s] - 3
        state = []
        for s in range(n_sub):
            write(s, done[s][1])
            state.append((qs[s], nxt[s], *done[s]))

        worst = functools.reduce(jnp.maximum, [c for _, _, c, _ in state])

        @pl.when(jnp.max(worst) > SB_DEAD_LOG2_WEIGHT)
        def _():
            for s, (q, nxt, carry, acc) in enumerate(state):
                def cond(st):
                    j, alive, _, _ = st
                    return jnp.logical_and(j >= 0, alive)

                def body(st, q=q):
                    j, _, carry, acc = st
                    ((carry, acc),) = _sb_tiles([(q, *kv_blocks(j, 1), carry, acc, None)], later_m)
                    return j - 1, jnp.max(carry) > SB_DEAD_LOG2_WEIGHT, carry, acc

                alive = jnp.max(carry) > SB_DEAD_LOG2_WEIGHT
                _, _, _, acc = lax.while_loop(cond, body, (nxt, alive, carry, acc))
                write(s, acc)

    @pl.when(qi == 0)
    def _():
        step(True)

    @pl.when(qi > 0)
    def _():
        step(False)


def _stick_breaking(u, b, tq, tr):
    four_h, rows, d = u.shape
    heads = four_h // 4
    dm = heads * d
    s = rows // b
    nq = s // tq
    assert s >= 2 * tr and tq % tr == 0
    row = jnp.arange(2 * tr)
    later_m = (row[:, None] > row[None, :]).astype(BF16)
    blocks = 2 * _nbytes((s, d), BF16) + 3 * _nbytes((tq, d), BF16)
    resident = _nbytes(later_m.shape, BF16) + (tq // tr) * 12 * _nbytes((tr, 2 * tr), F32)
    return pl.pallas_call(
        functools.partial(_sb_kernel, n_sub=tq // tr, tr=tr),
        grid=(b, heads, nq),
        in_specs=[
            pl.BlockSpec((1, tq, d), lambda bi, hi, qi: (hi, bi * nq + qi, 0)),
            pl.BlockSpec((1, s, d), lambda bi, hi, qi: (heads + hi, bi, 0)),
            pl.BlockSpec((1, s, d), lambda bi, hi, qi: (2 * heads + hi, bi, 0)),
            pl.BlockSpec((1, tq, d), lambda bi, hi, qi: (3 * heads + hi, bi * nq + qi, 0)),
            pl.BlockSpec(later_m.shape, lambda bi, hi, qi: (0, 0)),
        ],
        out_specs=pl.BlockSpec((1, tq, d), lambda bi, hi, qi: (bi, qi, hi)),
        out_shape=jax.ShapeDtypeStruct((b, s, dm), BF16),
        compiler_params=pltpu.CompilerParams(
            dimension_semantics=("arbitrary", "arbitrary", "arbitrary"),
            vmem_limit_bytes=_vmem_limit(blocks, resident)),
        name="stick_breaking",
    )(u, u, u, u, later_m)


def _xattn_kernel(q_ref, k_ref, v_ref, o_ref, *, heads):
    dm = q_ref.shape[2]
    hd = dm // heads
    scale = 1.0 / math.sqrt(hd)
    cols = [slice(h * hd, (h + 1) * hd) for h in range(heads)]
    s = [lax.dot_general(q_ref[0, :, c], k_ref[0, :, c], (((1,), (1,)), ((), ())),
                         preferred_element_type=F32) * scale for c in cols]
    e = [jnp.exp(x - jnp.max(x, axis=-1, keepdims=True)) for x in s]
    p = [(x * (1.0 / jnp.sum(x, axis=-1, keepdims=True))).astype(BF16) for x in e]
    o = [jnp.dot(x, v_ref[0, :, c], preferred_element_type=F32) for x, c in zip(p, cols)]
    for x, c in zip(o, cols):
        o_ref[0, :, c] = x.astype(o_ref.dtype)


def _cross_attention_core(q, kv, t):
    b, s, dm = q.shape
    mlen = kv.shape[1]
    blocks = 2 * _nbytes((t, dm), BF16) + 2 * _nbytes((mlen, dm), BF16)
    resident = 6 * _nbytes((t, mlen), F32) + 2 * _nbytes((t, dm // XA_HEADS), F32)
    return pl.pallas_call(
        functools.partial(_xattn_kernel, heads=XA_HEADS),
        grid=(b, s // t),
        in_specs=[
            pl.BlockSpec((1, t, dm), lambda bi, si: (bi, si, 0)),
            pl.BlockSpec((1, mlen, dm), lambda bi, si: (bi, 0, 0)),
            pl.BlockSpec((1, mlen, dm), lambda bi, si: (bi, 0, 1)),
        ],
        out_specs=pl.BlockSpec((1, t, dm), lambda bi, si: (bi, si, 0)),
        out_shape=jax.ShapeDtypeStruct((b, s, dm), BF16),
        compiler_params=pltpu.CompilerParams(
            dimension_semantics=("arbitrary", "arbitrary"),
            vmem_limit_bytes=_vmem_limit(blocks, resident)),
        name="cross_attention",
    )(q, kv, kv)


MM_TM = 1024
MM_TN = 1024
MM_TN_BF16_OUT = 2048
RES_TM = 512
POOL_ROWS = 512
HGRN_ROWS = 512
HGRN_BLOCK = 256
HGRN_HEADS_PER_STEP = 4
SB_Q_ROWS = 1024
SB_K_ROWS = 128
XA_ROWS = 512


def kernel(x, mem, norm_mix_g, norm_xa_g, norm_mem_g, final_norm_g, w_in_even, pool_w,
           pool_scale, hgrn_lower_bounds, hgrn_norm_g, w_out_even, w_in_odd, w_out_odd,
           xa_wq, xa_wkv, xa_wo):
    b, s, dm = x.shape
    mlen = mem.shape[1]
    depth = norm_mix_g.shape[0]
    rows = b * s
    xs = x.reshape(rows, dm)
    mem2 = mem.reshape(b * mlen, dm)
    sb_col_scale = jnp.concatenate([jnp.full((dm,), LOG2E / math.sqrt(SB_HEAD_DIM), F32),
                                    jnp.ones((3 * dm,), F32)])
    w_in_even_b, w_out_even_b = w_in_even.astype(BF16), w_out_even.astype(BF16)
    w_in_odd_b, w_out_odd_b = (w_in_odd * sb_col_scale).astype(BF16), w_out_odd.astype(BF16)
    wq_b, wkv_b, wo_b = xa_wq.astype(BF16), xa_wkv.astype(BF16), xa_wo.astype(BF16)
    h = None
    for l in range(depth):
        if l % 2 == 0:
            e = l // 2
            if h is None:
                u = _norm_matmul(xs, norm_mix_g[l], w_in_even_b, e, F32, MM_TM, MM_TN)
            else:
                u = _matmul(h, w_in_even_b, e, F32, MM_TM, MM_TN)
            u = u.reshape(b, s, -1)
            y_a = _pool_mixer(u, pool_w[e].astype(BF16), pool_scale[e], POOL_ROWS)
            y_b = _hgrn2(u, hgrn_lower_bounds, hgrn_norm_g[e], e, HGRN_ROWS,
                         HGRN_HEADS_PER_STEP)
            mixed = [y_a.reshape(rows, -1), y_b.reshape(rows, -1)]
            w_out, w_out_layer = w_out_even_b, e
        else:
            o = l // 2
            assert h is not None, "stick-breaking layers follow a layer that emits h"
            u = _matmul(h, w_in_odd_b, o, BF16, MM_TM, MM_TN_BF16_OUT, head_dim=SB_HEAD_DIM)
            mixed = [_stick_breaking(u, b, SB_Q_ROWS, SB_K_ROWS).reshape(rows, dm)]
            w_out, w_out_layer = w_out_odd_b, o
        xs, h_xa = _matmul_residual(mixed, w_out, w_out_layer, xs, norm_xa_g[l], RES_TM, "emit")
        q = _matmul(h_xa, wq_b, l, BF16, MM_TM, MM_TN_BF16_OUT)
        kv = _norm_matmul(mem2, norm_mem_g[l], wkv_b, l, BF16, b * mlen, MM_TN)
        att = _cross_attention_core(q.reshape(b, s, dm), kv.reshape(b, mlen, 2 * dm), XA_ROWS)
        att = [att.reshape(rows, dm)]
        if l == depth - 1:
            (xs,) = _matmul_residual(att, wo_b, l, xs, final_norm_g, RES_TM, "final")
        else:
            xs, h = _matmul_residual(att, wo_b, l, xs, norm_mix_g[l + 1], RES_TM, "emit")
    return xs.reshape(b, s, dm)
```

```python
import functools
import math

import jax
import jax.numpy as jnp
from jax import lax
from jax.experimental import pallas as pl
from jax.experimental.pallas import tpu as pltpu

F32 = jnp.float32
BF16 = jnp.bfloat16

EPS = 1e-6
LOG2E = math.log2(math.e)
POOL_WINDOWS = (2, 4, 8, 16)
POOL_HALO = 32
F32_SUBLANES = 8
HGRN_HEAD_DIM = 128
HGRN_CHUNK = 32
BF16_SUBLANES = 16
SB_HEAD_DIM = 128
XA_HEADS = 4

V7X_VMEM_BYTES = 64 * 1024 * 1024
VMEM_CAP_BYTES = V7X_VMEM_BYTES - 8 * 1024 * 1024
COMPILER_SCRATCH_BYTES = 8 * 1024 * 1024

SB_DEAD_LOG2_WEIGHT = -151.0
SB_MASKED_LOGIT = -1e30


def _vmem_limit(pipelined_bytes, resident_bytes=0):
    need = 2 * pipelined_bytes + resident_bytes + COMPILER_SCRATCH_BYTES
    return int(min(VMEM_CAP_BYTES, need))


def _nbytes(shape, dtype):
    return math.prod(shape) * jnp.dtype(dtype).itemsize


def _sigmoid(x):
    return 0.5 * jnp.tanh(0.5 * x) + 0.5


def _silu(x):
    return x * _sigmoid(x)


def _split_bf16(x):
    hi = x.astype(BF16)
    lo = (x - hi.astype(F32)).astype(BF16)
    return hi, lo


def _rms_scale(x):
    return x * lax.rsqrt(jnp.mean(x * x, axis=-1, keepdims=True) + EPS)


NORM_ROWS = 256


def _norm_matmul_kernel(x_ref, g_ref, w_ref, o_ref, h_ref):
    tm = x_ref.shape[0]

    @pl.when(pl.program_id(1) == 0)
    def _():
        g = g_ref[...]
        for r in range(0, tm, NORM_ROWS):
            rows = slice(r, r + NORM_ROWS)
            h = (_rms_scale(x_ref[rows, :]) * g).astype(h_ref.dtype)
            h_ref[rows, :] = h
            o_ref[rows, :] = jnp.dot(h, w_ref[...],
                                     preferred_element_type=F32).astype(o_ref.dtype)

    @pl.when(pl.program_id(1) > 0)
    def _():
        o_ref[...] = jnp.dot(h_ref[...], w_ref[...],
                             preferred_element_type=F32).astype(o_ref.dtype)


def _norm_matmul(x, g, w, layer, out_dtype, tm, tn):
    m, k = x.shape
    n = w.shape[2]
    assert m % tm == 0 and n % tn == 0 and tm % NORM_ROWS == 0
    blocks = (_nbytes((tm, k), x.dtype) + _nbytes((k, tn), w.dtype)
              + _nbytes((tm, tn), out_dtype))
    scratch = _nbytes((tm, k), BF16) + _nbytes((tm, tn), F32)
    return pl.pallas_call(
        _norm_matmul_kernel,
        grid=(m // tm, n // tn),
        in_specs=[
            pl.BlockSpec((tm, k), lambda i, j: (i, 0)),
            pl.BlockSpec((1, k), lambda i, j: (0, 0)),
            pl.BlockSpec((None, k, tn), lambda i, j: (layer, 0, j)),
        ],
        out_specs=pl.BlockSpec((tm, tn), lambda i, j: (i, j)),
        out_shape=jax.ShapeDtypeStruct((m, n), out_dtype),
        scratch_shapes=[pltpu.VMEM((tm, k), BF16)],
        compiler_params=pltpu.CompilerParams(
            dimension_semantics=("arbitrary", "arbitrary"),
            vmem_limit_bytes=_vmem_limit(blocks, scratch)),
        name="norm_matmul",
    )(x, g.reshape(1, k), w)


def _matmul_kernel(a_ref, w_ref, o_ref):
    res = jnp.dot(a_ref[...], w_ref[...], preferred_element_type=F32).astype(o_ref.dtype)
    if len(o_ref.shape) == 2:
        o_ref[...] = res
    else:
        head_dim = o_ref.shape[2]
        for hh in range(o_ref.shape[0]):
            o_ref[hh] = res[:, hh * head_dim:(hh + 1) * head_dim]


def _matmul(a, w, layer, out_dtype, tm, tn, head_dim=None):
    m, k = a.shape
    n = w.shape[2]
    assert m % tm == 0 and n % tn == 0
    blocks = (_nbytes((tm, k), a.dtype) + _nbytes((k, tn), w.dtype)
              + _nbytes((tm, tn), out_dtype))
    if head_dim is None:
        out_spec = pl.BlockSpec((tm, tn), lambda i, j: (i, j))
        out_shape = jax.ShapeDtypeStruct((m, n), out_dtype)
    else:
        out_spec = pl.BlockSpec((tn // head_dim, tm, head_dim), lambda i, j: (j, i, 0))
        out_shape = jax.ShapeDtypeStruct((n // head_dim, m, head_dim), out_dtype)
    return pl.pallas_call(
        _matmul_kernel,
        grid=(m // tm, n // tn),
        in_specs=[pl.BlockSpec((tm, k), lambda i, j: (i, 0)),
                  pl.BlockSpec((None, k, tn), lambda i, j: (layer, 0, j))],
        out_specs=out_spec,
        out_shape=out_shape,
        compiler_params=pltpu.CompilerParams(
            dimension_semantics=("arbitrary", "arbitrary"),
            vmem_limit_bytes=_vmem_limit(blocks, _nbytes((tm, tn), F32))),
        name="matmul",
    )(a, w)


def _matmul_residual_kernel(*refs, n_lhs, norm_mode):
    lhs_refs = refs[:n_lhs]
    w_refs = refs[n_lhs:2 * n_lhs]
    res_ref, g_ref = refs[2 * n_lhs], refs[2 * n_lhs + 1]
    out_refs = refs[2 * n_lhs + 2:]
    acc = res_ref[...]
    for a_ref, w_ref in zip(lhs_refs, w_refs):
        acc = acc + jnp.dot(a_ref[...], w_ref[...], preferred_element_type=F32)
    normed = _rms_scale(acc) * g_ref[...]
    if norm_mode == "final":
        out_refs[0][...] = normed
    else:
        out_refs[0][...] = acc
        out_refs[1][...] = normed.astype(out_refs[1].dtype)


def _matmul_residual(lhs_list, w, layer, res, g, tm, norm_mode):
    assert norm_mode in ("emit", "final")
    m, n = res.shape
    kp = lhs_list[0].shape[1]
    assert all(a.shape == (m, kp) for a in lhs_list)
    assert w.shape[1:] == (kp * len(lhs_list), n)
    n_lhs = len(lhs_list)
    row_block = pl.BlockSpec((tm, n), lambda i: (i, 0))
    in_specs = [pl.BlockSpec((tm, kp), lambda i: (i, 0)) for _ in lhs_list]
    in_specs += [pl.BlockSpec((None, kp, n), functools.partial(lambda i, p: (layer, p, 0), p=p))
                 for p in range(n_lhs)]
    in_specs += [row_block, pl.BlockSpec((1, n), lambda i: (0, 0))]
    out_shape = [jax.ShapeDtypeStruct((m, n), F32)]
    if norm_mode == "emit":
        out_shape += [jax.ShapeDtypeStruct((m, n), BF16)]
    blocks = (n_lhs * (_nbytes((tm, kp), BF16) + _nbytes((kp, n), BF16))
              + 2 * _nbytes((tm, n), F32) + _nbytes((tm, n), BF16))
    return pl.pallas_call(
        functools.partial(_matmul_residual_kernel, n_lhs=n_lhs, norm_mode=norm_mode),
        grid=(m // tm,),
        in_specs=in_specs,
        out_specs=[row_block] * len(out_shape),
        out_shape=out_shape,
        compiler_params=pltpu.CompilerParams(
            dimension_semantics=("arbitrary",),
            vmem_limit_bytes=_vmem_limit(blocks, 2 * _nbytes((tm, n), F32))),
        name="matmul_residual",
    )(*lhs_list, *([w] * n_lhs), res, g.reshape(1, n))


def _pool_kernel(a_ref, gate_ref, w_ref, scale_ref, o_ref, buf_ref, part_a, part_b, *, group):
    t = a_ref.shape[1]
    end = POOL_HALO + t
    sblk = pl.program_id(1)

    @pl.when(sblk == 0)
    def _():
        buf_ref[0:POOL_HALO, :] = jnp.zeros((POOL_HALO, buf_ref.shape[1]), F32)

    buf_ref[POOL_HALO:end, :] = a_ref[0].astype(F32)
    pos = sblk * t + lax.broadcasted_iota(jnp.int32, (t, 1), 0)
    for gi, win in enumerate(POOL_WINDOWS):
        cols = slice(gi * group, (gi + 1) * group)
        cur = buf_ref[POOL_HALO:end, cols]
        levels = win.bit_length() - 1
        load = lambda lo, hi, cols=cols: buf_ref[lo:hi, cols]
        for k in range(1, levels + 1):
            shift = 2 ** (k - 1)
            lo = POOL_HALO - F32_SUBLANES * (levels - k)
            tot = load(lo, end) + load(lo - shift, end - shift)
            if k < levels:
                part = part_a if k % 2 else part_b
                part[lo:end, :] = tot
                load = lambda lo, hi, part=part: part[lo:hi, :]
        cnt = jnp.minimum(pos + 1, win).astype(F32)
        mixed = tot / cnt - cur
        y = jnp.dot(mixed.astype(BF16), w_ref[gi], preferred_element_type=F32)
        y = y * scale_ref[:, cols]
        o_ref[0, :, cols] = (y * _silu(gate_ref[0, :, cols].astype(F32))).astype(o_ref.dtype)
    buf_ref[0:POOL_HALO, :] = buf_ref[t:t + POOL_HALO, :]


def _pool_mixer(u, pool_w, pool_scale, t):
    b, s, _ = u.shape
    n_groups, group, _ = pool_w.shape
    width = n_groups * group
    blocks = 2 * _nbytes((t, width), u.dtype) + _nbytes((t, width), BF16)
    assert POOL_HALO >= F32_SUBLANES * (max(POOL_WINDOWS).bit_length() - 1)
    resident = (_nbytes((t + POOL_HALO, width), F32) + _nbytes(pool_w.shape, BF16)
                + 6 * _nbytes((t + POOL_HALO, group), F32))
    return pl.pallas_call(
        functools.partial(_pool_kernel, group=group),
        grid=(b, s // t),
        in_specs=[
            pl.BlockSpec((1, t, width), lambda bi, si: (bi, si, 0)),
            pl.BlockSpec((1, t, width), lambda bi, si: (bi, si, 1)),
            pl.BlockSpec(pool_w.shape, lambda bi, si: (0, 0, 0)),
            pl.BlockSpec((1, width), lambda bi, si: (0, 0)),
        ],
        out_specs=pl.BlockSpec((1, t, width), lambda bi, si: (bi, si, 0)),
        out_shape=jax.ShapeDtypeStruct((b, s, width), BF16),
        scratch_shapes=[pltpu.VMEM((t + POOL_HALO, width), F32),
                        pltpu.VMEM((t + POOL_HALO, group), F32),
                        pltpu.VMEM((t + POOL_HALO, group), F32)],
        compiler_params=pltpu.CompilerParams(
            dimension_semantics=("arbitrary", "arbitrary"),
            vmem_limit_bytes=_vmem_limit(blocks, resident)),
        name="pool_mixer",
    )(u, u, pool_w, pool_scale.reshape(1, width))


def _hgrn_kernel(q_ref, f_ref, i_ref, gate_ref, lbp_ref, ng_ref, cum_ref,
                 o_ref, state_ref, *, layer, heads_per_step):
    t = q_ref.shape[1]
    d = HGRN_HEAD_DIM
    gw = heads_per_step * d
    n_chunks = t // HGRN_CHUNK
    nt_dims = (((1,), (1,)), ((), ()))
    tn_dims = (((0,), (0,)), ((), ()))

    @pl.when(pl.program_id(2) == 0)
    def _():
        state_ref[...] = jnp.zeros(state_ref.shape, F32)

    lbp = lbp_ref[...]
    ex = jnp.exp(lbp - jnp.max(lbp, axis=0, keepdims=True))
    lb = jnp.sum(ex[:layer + 1], axis=0, keepdims=True) / jnp.sum(ex, axis=0, keepdims=True)

    f = lb + (1.0 - lb) * _sigmoid(f_ref[0].astype(F32))
    logf = jnp.log(f)
    k = 1.0 - f

    cum_m = cum_ref[...]
    blk = cum_m.shape[1]
    blk_chunks = blk // HGRN_CHUNK
    blocks = [slice(r, r + blk) for r in range(0, t, blk)]
    hi, lo = _split_bf16(logf)
    hilo = jnp.concatenate([hi, lo], axis=1)
    sums = [jnp.dot(cum_m, hilo[r], preferred_element_type=F32) for r in blocks]
    sums = [x[:, :gw] + x[:, gw:] for x in sums]
    bcum = jnp.concatenate([x[:blk] for x in sums], axis=0)
    decay = jnp.exp(jnp.concatenate([x[blk:blk + blk_chunks] for x in sums], axis=0))

    q_t = (q_ref[0].astype(F32) * jnp.exp(bcum)).astype(BF16)
    k_t = k * jnp.exp(-bcum)
    k_t_b = k_t.astype(BF16)
    inp_b = i_ref[0].astype(BF16)
    in_chunk = cum_m[:blk] > 0
    heads = [slice(h * d, (h + 1) * d) for h in range(heads_per_step)]
    chunks = [slice(n * HGRN_CHUNK, (n + 1) * HGRN_CHUNK) for n in range(n_chunks)]

    scores = [[lax.dot_general(q_t[r, c], k_t_b[r, c], nt_dims, preferred_element_type=F32)
               for r in blocks] for c in heads]
    scores = [[jnp.where(in_chunk, s, 0.0).astype(BF16) for s in per_head]
              for per_head in scores]
    o_intra = [jnp.concatenate([jnp.dot(s, inp_b[r, c], preferred_element_type=F32)
                                for s, r in zip(per_head, blocks)], axis=0)
               for per_head, c in zip(scores, heads)]

    k_end = [(k_t[r] * decay[n:n + 1, :]).astype(BF16) for n, r in enumerate(chunks)]
    incr = [[lax.dot_general(inp_b[r, c], k_end[n][:, c], tn_dims, preferred_element_type=F32)
             for c in heads] for n, r in enumerate(chunks)]
    states = []
    cur = [state_ref[h] for h in range(heads_per_step)]
    for n in range(n_chunks):
        states.append([s.astype(BF16) for s in cur])
        cur = [s * decay[n:n + 1, c] + u for s, c, u in zip(cur, heads, incr[n])]
    for h in range(heads_per_step):
        state_ref[h] = cur[h]

    o_inter = [jnp.concatenate(
        [lax.dot_general(q_t[r, c], states[n][h], nt_dims, preferred_element_type=F32)
         for n, r in enumerate(chunks)], axis=0) for h, c in enumerate(heads)]

    for h, c in enumerate(heads):
        o = _rms_scale(o_intra[h] + o_inter[h]) * ng_ref[:, c]
        o_ref[0, :, c] = (o * _silu(gate_ref[0, :, c].astype(F32))).astype(o_ref.dtype)


def _hgrn2(u, lower_bounds, norm_g, layer, t, heads_per_step):
    b, s, six_w = u.shape
    width = six_w // 6
    d = HGRN_HEAD_DIM
    gw = heads_per_step * d
    groups = width // gw
    blk = HGRN_BLOCK
    assert t % blk == 0 and blk // HGRN_CHUNK <= BF16_SUBLANES
    pos = jnp.arange(blk, dtype=jnp.int32)
    chunk_id = pos // HGRN_CHUNK
    tril = (chunk_id[:, None] == chunk_id[None, :]) & (pos[None, :] <= pos[:, None])
    select = jnp.arange(BF16_SUBLANES, dtype=jnp.int32)[:, None] == chunk_id[None, :]
    cum_m = jnp.concatenate([tril, select], axis=0).astype(BF16)
    col = lambda base: (lambda bi, gi, si: (bi, si, base * groups + gi))
    blocks = 4 * _nbytes((t, gw), u.dtype) + _nbytes((t, gw), BF16)
    resident = (2 * _nbytes(cum_m.shape, BF16) + _nbytes((heads_per_step, d, d), F32)
                + heads_per_step * (16 * _nbytes((t, d), F32) + 2 * _nbytes((t, blk), F32)))
    return pl.pallas_call(
        functools.partial(_hgrn_kernel, layer=layer, heads_per_step=heads_per_step),
        grid=(b, groups, s // t),
        in_specs=[
            pl.BlockSpec((1, t, gw), col(2)),
            pl.BlockSpec((1, t, gw), col(3)),
            pl.BlockSpec((1, t, gw), col(4)),
            pl.BlockSpec((1, t, gw), col(5)),
            pl.BlockSpec((lower_bounds.shape[0], gw), lambda bi, gi, si: (0, gi)),
            pl.BlockSpec((1, gw), lambda bi, gi, si: (0, gi)),
            pl.BlockSpec(cum_m.shape, lambda bi, gi, si: (0, 0)),
        ],
        out_specs=pl.BlockSpec((1, t, gw), lambda bi, gi, si: (bi, si, gi)),
        out_shape=jax.ShapeDtypeStruct((b, s, width), BF16),
        scratch_shapes=[pltpu.VMEM((heads_per_step, d, d), F32)],
        compiler_params=pltpu.CompilerParams(
            dimension_semantics=("arbitrary", "arbitrary", "arbitrary"),
            vmem_limit_bytes=_vmem_limit(blocks, resident)),
        name="hgrn2",
    )(u, u, u, u, lower_bounds, norm_g.reshape(1, width), cum_m)


def _sb_tiles(tiles, later_m):
    nt_dims = (((1,), (1,)), ((), ()))
    z = [lax.dot_general(q, kb, nt_dims, preferred_element_type=F32)
         for q, kb, _, _, _, _ in tiles]
    log_beta, log_keep_b, carry = [], [], []
    for x, t in zip(z, tiles):
        if t[5] is not None:
            x = jnp.where(t[5], x, SB_MASKED_LOGIT)
        soft = jnp.log(1.0 + jnp.exp2(-jnp.abs(x))) * LOG2E
        lb = jnp.minimum(x, 0.0) - soft
        lk = lb - x
        log_beta.append(lb)
        log_keep_b.append(lk.astype(BF16))
        carry.append(t[3] + jnp.sum(lk, axis=-1, keepdims=True))
    later = [jnp.dot(lk, later_m, preferred_element_type=F32) for lk in log_keep_b]
    a = [jnp.exp2(lb + lt + t[3]).astype(BF16) for lb, lt, t in zip(log_beta, later, tiles)]
    acc = [t[4] + jnp.dot(x, t[2], preferred_element_type=F32) for x, t in zip(a, tiles)]
    return list(zip(carry, acc))


def _sb_kernel(q_ref, k_ref, v_ref, gate_ref, m_ref, o_ref, *, n_sub, tr):
    d = q_ref.shape[2]
    qi = pl.program_id(2)
    later_wide = m_ref[...]
    later_m = later_wide[:tr, :tr]
    col = lax.broadcasted_iota(jnp.int32, (tr, 2 * tr), 1)
    row = lax.broadcasted_iota(jnp.int32, (tr, 2 * tr), 0)
    causal_window = col - tr < row
    causal_start = col < row

    def kv_blocks(j, n):
        rows = pl.ds(pl.multiple_of(j * tr, tr), n * tr)
        return k_ref[0, rows, :], v_ref[0, rows, :]

    def write(s, acc):
        rows = slice(s * tr, (s + 1) * tr)
        o_ref[0, rows, :] = (acc * _silu(gate_ref[0, rows, :].astype(F32))).astype(o_ref.dtype)

    def step(sequence_start):
        qs = [q_ref[0, s * tr:(s + 1) * tr, :] for s in range(n_sub)]
        blk = [qi * n_sub + s for s in range(n_sub)]
        zero = (jnp.zeros((tr, 1), F32), jnp.zeros((tr, d), F32))
        first = []
        for s in range(n_sub):
            if sequence_start and s == 0:
                first.append((qs[s], *kv_blocks(0, 2), *zero, causal_start))
            else:
                first.append((qs[s], *kv_blocks(blk[s] - 1, 2), *zero, causal_window))
        done = _sb_tiles(first, later_wide)
        nxt = [-1 if (sequence_start and s == 0) else blk[s] - 2 for s in range(n_sub)]
        has_second = [not (sequence_start and s < 2) for s in range(n_sub)]
        second = [(qs[s], *kv_blocks(blk[s] - 2, 1), *done[s], None)
                  for s in range(n_sub) if has_second[s]]
        for s, res in zip([s for s in range(n_sub) if has_second[s]], _sb_tiles(second, later_m)):
            done[s] = res
            nxt[s] = blk[s] - 3
        state = []
        for s in range(n_sub):
            write(s, done[s][1])
            state.append((qs[s], nxt[s], *done[s]))

        worst = functools.reduce(jnp.maximum, [c for _, _, c, _ in state])

        @pl.when(jnp.max(worst) > SB_DEAD_LOG2_WEIGHT)
        def _():
            for s, (q, nxt, carry, acc) in enumerate(state):
                def cond(st):
                    j, alive, _, _ = st
                    return jnp.logical_and(j >= 0, alive)

                def body(st, q=q):
                    j, _, carry, acc = st
                    ((carry, acc),) = _sb_tiles([(q, *kv_blocks(j, 1), carry, acc, None)], later_m)
                    return j - 1, jnp.max(carry) > SB_DEAD_LOG2_WEIGHT, carry, acc

                alive = jnp.max(carry) > SB_DEAD_LOG2_WEIGHT
                _, _, _, acc = lax.while_loop(cond, body, (nxt, alive, carry, acc))
                write(s, acc)

    @pl.when(qi == 0)
    def _():
        step(True)

    @pl.when(qi > 0)
    def _():
        step(False)


def _stick_breaking(u, b, tq, tr):
    four_h, rows, d = u.shape
    heads = four_h // 4
    dm = heads * d
    s = rows // b
    nq = s // tq
    assert s >= 2 * tr and tq % tr == 0
    row = jnp.arange(2 * tr)
    later_m = (row[:, None] > row[None, :]).astype(BF16)
    blocks = 2 * _nbytes((s, d), BF16) + 3 * _nbytes((tq, d), BF16)
    resident = _nbytes(later_m.shape, BF16) + (tq // tr) * 12 * _nbytes((tr, 2 * tr), F32)
    return pl.pallas_call(
        functools.partial(_sb_kernel, n_sub=tq // tr, tr=tr),
        grid=(b, heads, nq),
        in_specs=[
            pl.BlockSpec((1, tq, d), lambda bi, hi, qi: (hi, bi * nq + qi, 0)),
            pl.BlockSpec((1, s, d), lambda bi, hi, qi: (heads + hi, bi, 0)),
            pl.BlockSpec((1, s, d), lambda bi, hi, qi: (2 * heads + hi, bi, 0)),
            pl.BlockSpec((1, tq, d), lambda bi, hi, qi: (3 * heads + hi, bi * nq + qi, 0)),
            pl.BlockSpec(later_m.shape, lambda bi, hi, qi: (0, 0)),
        ],
        out_specs=pl.BlockSpec((1, tq, d), lambda bi, hi, qi: (bi, qi, hi)),
        out_shape=jax.ShapeDtypeStruct((b, s, dm), BF16),
        compiler_params=pltpu.CompilerParams(
            dimension_semantics=("arbitrary", "arbitrary", "arbitrary"),
            vmem_limit_bytes=_vmem_limit(blocks, resident)),
        name="stick_breaking",
    )(u, u, u, u, later_m)


def _xattn_kernel(q_ref, k_ref, v_ref, o_ref, *, heads):
    dm = q_ref.shape[2]
    hd = dm // heads
    scale = 1.0 / math.sqrt(hd)
    cols = [slice(h * hd, (h + 1) * hd) for h in range(heads)]
    s = [lax.dot_general(q_ref[0, :, c], k_ref[0, :, c], (((1,), (1,)), ((), ())),
                         preferred_element_type=F32) * scale for c in cols]
    e = [jnp.exp(x - jnp.max(x, axis=-1, keepdims=True)) for x in s]
    p = [(x * (1.0 / jnp.sum(x, axis=-1, keepdims=True))).astype(BF16) for x in e]
    o = [jnp.dot(x, v_ref[0, :, c], preferred_element_type=F32) for x, c in zip(p, cols)]
    for x, c in zip(o, cols):
        o_ref[0, :, c] = x.astype(o_ref.dtype)


def _cross_attention_core(q, kv, t):
    b, s, dm = q.shape
    mlen = kv.shape[1]
    blocks = 2 * _nbytes((t, dm), BF16) + 2 * _nbytes((mlen, dm), BF16)
    resident = 6 * _nbytes((t, mlen), F32) + 2 * _nbytes((t, dm // XA_HEADS), F32)
    return pl.pallas_call(
        functools.partial(_xattn_kernel, heads=XA_HEADS),
        grid=(b, s // t),
        in_specs=[
            pl.BlockSpec((1, t, dm), lambda bi, si: (bi, si, 0)),
            pl.BlockSpec((1, mlen, dm), lambda bi, si: (bi, 0, 0)),
            pl.BlockSpec((1, mlen, dm), lambda bi, si: (bi, 0, 1)),
        ],
        out_specs=pl.BlockSpec((1, t, dm), lambda bi, si: (bi, si, 0)),
        out_shape=jax.ShapeDtypeStruct((b, s, dm), BF16),
        compiler_params=pltpu.CompilerParams(
            dimension_semantics=("arbitrary", "arbitrary"),
            vmem_limit_bytes=_vmem_limit(blocks, resident)),
        name="cross_attention",
    )(q, kv, kv)


MM_TM = 1024
MM_TN = 1024
MM_TN_BF16_OUT = 2048
RES_TM = 512
POOL_ROWS = 512
HGRN_ROWS = 512
HGRN_BLOCK = 256
HGRN_HEADS_PER_STEP = 4
SB_Q_ROWS = 1024
SB_K_ROWS = 128
XA_ROWS = 512


def kernel(x, mem, norm_mix_g, norm_xa_g, norm_mem_g, final_norm_g, w_in_even, pool_w,
           pool_scale, hgrn_lower_bounds, hgrn_norm_g, w_out_even, w_in_odd, w_out_odd,
           xa_wq, xa_wkv, xa_wo):
    b, s, dm = x.shape
    mlen = mem.shape[1]
    depth = norm_mix_g.shape[0]
    rows = b * s
    xs = x.reshape(rows, dm)
    mem2 = mem.reshape(b * mlen, dm)
    sb_col_scale = jnp.concatenate([jnp.full((dm,), LOG2E / math.sqrt(SB_HEAD_DIM), F32),
                                    jnp.ones((3 * dm,), F32)])
    w_in_even_b, w_out_even_b = w_in_even.astype(BF16), w_out_even.astype(BF16)
    w_in_odd_b, w_out_odd_b = (w_in_odd * sb_col_scale).astype(BF16), w_out_odd.astype(BF16)
    wq_b, wkv_b, wo_b = xa_wq.astype(BF16), xa_wkv.astype(BF16), xa_wo.astype(BF16)
    h = None
    for l in range(depth):
        if l % 2 == 0:
            e = l // 2
            if h is None:
                u = _norm_matmul(xs, norm_mix_g[l], w_in_even_b, e, BF16, MM_TM, MM_TN_BF16_OUT)
            else:
                u = _matmul(h, w_in_even_b, e, BF16, MM_TM, MM_TN_BF16_OUT)
            u = u.reshape(b, s, -1)
            y_a = _pool_mixer(u, pool_w[e].astype(BF16), pool_scale[e], POOL_ROWS)
            y_b = _hgrn2(u, hgrn_lower_bounds, hgrn_norm_g[e], e, HGRN_ROWS,
                         HGRN_HEADS_PER_STEP)
            mixed = [y_a.reshape(rows, -1), y_b.reshape(rows, -1)]
            w_out, w_out_layer = w_out_even_b, e
        else:
            o = l // 2
            assert h is not None, "stick-breaking layers follow a layer that emits h"
            u = _matmul(h, w_in_odd_b, o, BF16, MM_TM, MM_TN_BF16_OUT, head_dim=SB_HEAD_DIM)
            mixed = [_stick_breaking(u, b, SB_Q_ROWS, SB_K_ROWS).reshape(rows, dm)]
            w_out, w_out_layer = w_out_odd_b, o
        xs, h_xa = _matmul_residual(mixed, w_out, w_out_layer, xs, norm_xa_g[l], RES_TM, "emit")
        q = _matmul(h_xa, wq_b, l, BF16, MM_TM, MM_TN_BF16_OUT)
        kv = _norm_matmul(mem2, norm_mem_g[l], wkv_b, l, BF16, b * mlen, MM_TN)
        att = _cross_attention_core(q.reshape(b, s, dm), kv.reshape(b, mlen, 2 * dm), XA_ROWS)
        att = [att.reshape(rows, dm)]
        if l == depth - 1:
            (xs,) = _matmul_residual(att, wo_b, l, xs, final_norm_g, RES_TM, "final")
        else:
            xs, h = _matmul_residual(att, wo_b, l, xs, norm_mix_g[l + 1], RES_TM, "emit")
    return xs.reshape(b, s, dm)
```

```python
import functools
import math

import jax
import jax.numpy as jnp
from jax import lax
from jax.experimental import pallas as pl
from jax.experimental.pallas import tpu as pltpu

F32 = jnp.float32
BF16 = jnp.bfloat16

EPS = 1e-6
LOG2E = math.log2(math.e)
POOL_WINDOWS = (2, 4, 8, 16)
POOL_HALO = 32
F32_SUBLANES = 8
HGRN_HEAD_DIM = 128
HGRN_CHUNK = 32
BF16_SUBLANES = 16
SB_HEAD_DIM = 128
XA_HEADS = 4

V7X_VMEM_BYTES = 64 * 1024 * 1024
VMEM_CAP_BYTES = V7X_VMEM_BYTES - 8 * 1024 * 1024
COMPILER_SCRATCH_BYTES = 8 * 1024 * 1024

SB_DEAD_LOG2_WEIGHT = -151.0
SB_MASKED_LOGIT = -1e30


def _vmem_limit(pipelined_bytes, resident_bytes=0):
    need = 2 * pipelined_bytes + resident_bytes + COMPILER_SCRATCH_BYTES
    return int(min(VMEM_CAP_BYTES, need))


def _nbytes(shape, dtype):
    return math.prod(shape) * jnp.dtype(dtype).itemsize


def _sigmoid(x):
    return 0.5 * jnp.tanh(0.5 * x) + 0.5


def _silu(x):
    return x * _sigmoid(x)


def _split_bf16(x):
    hi = x.astype(BF16)
    lo = (x - hi.astype(F32)).astype(BF16)
    return hi, lo


def _rms_scale(x):
    return x * lax.rsqrt(jnp.mean(x * x, axis=-1, keepdims=True) + EPS)


NORM_ROWS = 256


def _norm_matmul_kernel(x_ref, g_ref, w_ref, o_ref, h_ref):
    tm = x_ref.shape[0]

    @pl.when(pl.program_id(1) == 0)
    def _():
        g = g_ref[...]
        w = w_ref[...].astype(h_ref.dtype)
        for r in range(0, tm, NORM_ROWS):
            rows = slice(r, r + NORM_ROWS)
            h = (_rms_scale(x_ref[rows, :]) * g).astype(h_ref.dtype)
            h_ref[rows, :] = h
            o_ref[rows, :] = jnp.dot(h, w, preferred_element_type=F32).astype(o_ref.dtype)

    @pl.when(pl.program_id(1) > 0)
    def _():
        o_ref[...] = jnp.dot(h_ref[...], w_ref[...].astype(h_ref.dtype),
                             preferred_element_type=F32).astype(o_ref.dtype)


def _norm_matmul(x, g, w, layer, out_dtype, tm, tn):
    m, k = x.shape
    n = w.shape[2]
    assert m % tm == 0 and n % tn == 0 and tm % NORM_ROWS == 0
    blocks = (_nbytes((tm, k), x.dtype) + _nbytes((k, tn), w.dtype)
              + _nbytes((tm, tn), out_dtype))
    scratch = _nbytes((tm, k), BF16) + _nbytes((tm, tn), F32)
    return pl.pallas_call(
        _norm_matmul_kernel,
        grid=(m // tm, n // tn),
        in_specs=[
            pl.BlockSpec((tm, k), lambda i, j: (i, 0)),
            pl.BlockSpec((1, k), lambda i, j: (0, 0)),
            pl.BlockSpec((None, k, tn), lambda i, j: (layer, 0, j)),
        ],
        out_specs=pl.BlockSpec((tm, tn), lambda i, j: (i, j)),
        out_shape=jax.ShapeDtypeStruct((m, n), out_dtype),
        scratch_shapes=[pltpu.VMEM((tm, k), BF16)],
        compiler_params=pltpu.CompilerParams(
            dimension_semantics=("arbitrary", "arbitrary"),
            vmem_limit_bytes=_vmem_limit(blocks, scratch)),
        name="norm_matmul",
    )(x, g.reshape(1, k), w)


def _matmul_kernel(a_ref, w_ref, o_ref):
    res = jnp.dot(a_ref[...], w_ref[...], preferred_element_type=F32).astype(o_ref.dtype)
    if len(o_ref.shape) == 2:
        o_ref[...] = res
    else:
        head_dim = o_ref.shape[2]
        for hh in range(o_ref.shape[0]):
            o_ref[hh] = res[:, hh * head_dim:(hh + 1) * head_dim]


def _matmul(a, w, layer, out_dtype, tm, tn, head_dim=None):
    m, k = a.shape
    n = w.shape[2]
    assert m % tm == 0 and n % tn == 0
    blocks = (_nbytes((tm, k), a.dtype) + _nbytes((k, tn), w.dtype)
              + _nbytes((tm, tn), out_dtype))
    if head_dim is None:
        out_spec = pl.BlockSpec((tm, tn), lambda i, j: (i, j))
        out_shape = jax.ShapeDtypeStruct((m, n), out_dtype)
    else:
        out_spec = pl.BlockSpec((tn // head_dim, tm, head_dim), lambda i, j: (j, i, 0))
        out_shape = jax.ShapeDtypeStruct((n // head_dim, m, head_dim), out_dtype)
    return pl.pallas_call(
        _matmul_kernel,
        grid=(m // tm, n // tn),
        in_specs=[pl.BlockSpec((tm, k), lambda i, j: (i, 0)),
                  pl.BlockSpec((None, k, tn), lambda i, j: (layer, 0, j))],
        out_specs=out_spec,
        out_shape=out_shape,
        compiler_params=pltpu.CompilerParams(
            dimension_semantics=("arbitrary", "arbitrary"),
            vmem_limit_bytes=_vmem_limit(blocks, _nbytes((tm, tn), F32))),
        name="matmul",
    )(a, w)


def _matmul_residual_kernel(*refs, n_lhs, norm_mode):
    lhs_refs = refs[:n_lhs]
    w_refs = refs[n_lhs:2 * n_lhs]
    res_ref, g_ref = refs[2 * n_lhs], refs[2 * n_lhs + 1]
    out_refs = refs[2 * n_lhs + 2:]
    acc = res_ref[...]
    for a_ref, w_ref in zip(lhs_refs, w_refs):
        acc = acc + jnp.dot(a_ref[...], w_ref[...], preferred_element_type=F32)
    normed = _rms_scale(acc) * g_ref[...]
    if norm_mode == "final":
        out_refs[0][...] = normed
    else:
        out_refs[0][...] = acc
        out_refs[1][...] = normed.astype(out_refs[1].dtype)


def _matmul_residual(lhs_list, w, layer, res, g, tm, norm_mode):
    assert norm_mode in ("emit", "final")
    m, n = res.shape
    kp = lhs_list[0].shape[1]
    assert all(a.shape == (m, kp) for a in lhs_list)
    assert w.shape[1:] == (kp * len(lhs_list), n)
    n_lhs = len(lhs_list)
    row_block = pl.BlockSpec((tm, n), lambda i: (i, 0))
    in_specs = [pl.BlockSpec((tm, kp), lambda i: (i, 0)) for _ in lhs_list]
    in_specs += [pl.BlockSpec((None, kp, n), functools.partial(lambda i, p: (layer, p, 0), p=p))
                 for p in range(n_lhs)]
    in_specs += [row_block, pl.BlockSpec((1, n), lambda i: (0, 0))]
    out_shape = [jax.ShapeDtypeStruct((m, n), F32)]
    if norm_mode == "emit":
        out_shape += [jax.ShapeDtypeStruct((m, n), BF16)]
    blocks = (n_lhs * (_nbytes((tm, kp), BF16) + _nbytes((kp, n), BF16))
              + 2 * _nbytes((tm, n), F32) + _nbytes((tm, n), BF16))
    return pl.pallas_call(
        functools.partial(_matmul_residual_kernel, n_lhs=n_lhs, norm_mode=norm_mode),
        grid=(m // tm,),
        in_specs=in_specs,
        out_specs=[row_block] * len(out_shape),
        out_shape=out_shape,
        compiler_params=pltpu.CompilerParams(
            dimension_semantics=("arbitrary",),
            vmem_limit_bytes=_vmem_limit(blocks, 2 * _nbytes((tm, n), F32))),
        name="matmul_residual",
    )(*lhs_list, *([w] * n_lhs), res, g.reshape(1, n))


def _pool_kernel(a_ref, gate_ref, w_ref, scale_ref, o_ref, buf_ref, part_a, part_b, *, group):
    t = a_ref.shape[1]
    end = POOL_HALO + t
    sblk = pl.program_id(1)

    @pl.when(sblk == 0)
    def _():
        buf_ref[0:POOL_HALO, :] = jnp.zeros((POOL_HALO, buf_ref.shape[1]), F32)

    buf_ref[POOL_HALO:end, :] = a_ref[0].astype(F32)
    pos = sblk * t + lax.broadcasted_iota(jnp.int32, (t, 1), 0)
    for gi, win in enumerate(POOL_WINDOWS):
        cols = slice(gi * group, (gi + 1) * group)
        cur = buf_ref[POOL_HALO:end, cols]
        levels = win.bit_length() - 1
        load = lambda lo, hi, cols=cols: buf_ref[lo:hi, cols]
        for k in range(1, levels + 1):
            shift = 2 ** (k - 1)
            lo = POOL_HALO - F32_SUBLANES * (levels - k)
            tot = load(lo, end) + load(lo - shift, end - shift)
            if k < levels:
                part = part_a if k % 2 else part_b
                part[lo:end, :] = tot
                load = lambda lo, hi, part=part: part[lo:hi, :]
        cnt = jnp.minimum(pos + 1, win).astype(F32)
        mixed = tot / cnt - cur
        y = jnp.dot(mixed.astype(BF16), w_ref[gi], preferred_element_type=F32)
        y = y * scale_ref[:, cols]
        o_ref[0, :, cols] = (y * _silu(gate_ref[0, :, cols].astype(F32))).astype(o_ref.dtype)
    buf_ref[0:POOL_HALO, :] = buf_ref[t:t + POOL_HALO, :]


def _pool_mixer(u, pool_w, pool_scale, t):
    b, s, _ = u.shape
    n_groups, group, _ = pool_w.shape
    width = n_groups * group
    blocks = 2 * _nbytes((t, width), u.dtype) + _nbytes((t, width), BF16)
    assert POOL_HALO >= F32_SUBLANES * (max(POOL_WINDOWS).bit_length() - 1)
    resident = (_nbytes((t + POOL_HALO, width), F32) + _nbytes(pool_w.shape, BF16)
                + 6 * _nbytes((t + POOL_HALO, group), F32))
    return pl.pallas_call(
        functools.partial(_pool_kernel, group=group),
        grid=(b, s // t),
        in_specs=[
            pl.BlockSpec((1, t, width), lambda bi, si: (bi, si, 0)),
            pl.BlockSpec((1, t, width), lambda bi, si: (bi, si, 1)),
            pl.BlockSpec(pool_w.shape, lambda bi, si: (0, 0, 0)),
            pl.BlockSpec((1, width), lambda bi, si: (0, 0)),
        ],
        out_specs=pl.BlockSpec((1, t, width), lambda bi, si: (bi, si, 0)),
        out_shape=jax.ShapeDtypeStruct((b, s, width), BF16),
        scratch_shapes=[pltpu.VMEM((t + POOL_HALO, width), F32),
                        pltpu.VMEM((t + POOL_HALO, group), F32),
                        pltpu.VMEM((t + POOL_HALO, group), F32)],
        compiler_params=pltpu.CompilerParams(
            dimension_semantics=("arbitrary", "arbitrary"),
            vmem_limit_bytes=_vmem_limit(blocks, resident)),
        name="pool_mixer",
    )(u, u, pool_w, pool_scale.reshape(1, width))


def _hgrn_kernel(q_ref, f_ref, i_ref, gate_ref, lbp_ref, ng_ref, cum_ref,
                 o_ref, state_ref, *, layer, heads_per_step):
    t = q_ref.shape[1]
    d = HGRN_HEAD_DIM
    gw = heads_per_step * d
    n_chunks = t // HGRN_CHUNK
    nt_dims = (((1,), (1,)), ((), ()))
    tn_dims = (((0,), (0,)), ((), ()))

    @pl.when(pl.program_id(2) == 0)
    def _():
        state_ref[...] = jnp.zeros(state_ref.shape, F32)

    lbp = lbp_ref[...]
    ex = jnp.exp(lbp - jnp.max(lbp, axis=0, keepdims=True))
    lb = jnp.sum(ex[:layer + 1], axis=0, keepdims=True) / jnp.sum(ex, axis=0, keepdims=True)

    f = lb + (1.0 - lb) * _sigmoid(f_ref[0].astype(F32))
    logf = jnp.log(f)
    k = 1.0 - f

    cum_m = cum_ref[...]
    blk = cum_m.shape[1]
    blk_chunks = blk // HGRN_CHUNK
    blocks = [slice(r, r + blk) for r in range(0, t, blk)]
    hi, lo = _split_bf16(logf)
    hilo = jnp.concatenate([hi, lo], axis=1)
    sums = [jnp.dot(cum_m, hilo[r], preferred_element_type=F32) for r in blocks]
    sums = [x[:, :gw] + x[:, gw:] for x in sums]
    bcum = jnp.concatenate([x[:blk] for x in sums], axis=0)
    decay = jnp.exp(jnp.concatenate([x[blk:blk + blk_chunks] for x in sums], axis=0))

    q_t = (q_ref[0].astype(F32) * jnp.exp(bcum)).astype(BF16)
    k_t = k * jnp.exp(-bcum)
    k_t_b = k_t.astype(BF16)
    inp_b = i_ref[0].astype(BF16)
    in_chunk = cum_m[:blk] > 0
    heads = [slice(h * d, (h + 1) * d) for h in range(heads_per_step)]
    chunks = [slice(n * HGRN_CHUNK, (n + 1) * HGRN_CHUNK) for n in range(n_chunks)]

    scores = [[lax.dot_general(q_t[r, c], k_t_b[r, c], nt_dims, preferred_element_type=F32)
               for r in blocks] for c in heads]
    scores = [[jnp.where(in_chunk, s, 0.0).astype(BF16) for s in per_head]
              for per_head in scores]
    o_intra = [jnp.concatenate([jnp.dot(s, inp_b[r, c], preferred_element_type=F32)
                                for s, r in zip(per_head, blocks)], axis=0)
               for per_head, c in zip(scores, heads)]

    k_end = [(k_t[r] * decay[n:n + 1, :]).astype(BF16) for n, r in enumerate(chunks)]
    incr = [[lax.dot_general(inp_b[r, c], k_end[n][:, c], tn_dims, preferred_element_type=F32)
             for c in heads] for n, r in enumerate(chunks)]
    states = []
    cur = [state_ref[h] for h in range(heads_per_step)]
    for n in range(n_chunks):
        states.append([s.astype(BF16) for s in cur])
        cur = [s * decay[n:n + 1, c] + u for s, c, u in zip(cur, heads, incr[n])]
    for h in range(heads_per_step):
        state_ref[h] = cur[h]

    o_inter = [jnp.concatenate(
        [lax.dot_general(q_t[r, c], states[n][h], nt_dims, preferred_element_type=F32)
         for n, r in enumerate(chunks)], axis=0) for h, c in enumerate(heads)]

    for h, c in enumerate(heads):
        o = _rms_scale(o_intra[h] + o_inter[h]) * ng_ref[:, c]
        o_ref[0, :, c] = (o * _silu(gate_ref[0, :, c].astype(F32))).astype(o_ref.dtype)


def _hgrn2(u, lower_bounds, norm_g, layer, t, heads_per_step):
    b, s, six_w = u.shape
    width = six_w // 6
    d = HGRN_HEAD_DIM
    gw = heads_per_step * d
    groups = width // gw
    blk = HGRN_BLOCK
    assert t % blk == 0 and blk // HGRN_CHUNK <= BF16_SUBLANES
    pos = jnp.arange(blk, dtype=jnp.int32)
    chunk_id = pos // HGRN_CHUNK
    tril = (chunk_id[:, None] == chunk_id[None, :]) & (pos[None, :] <= pos[:, None])
    select = jnp.arange(BF16_SUBLANES, dtype=jnp.int32)[:, None] == chunk_id[None, :]
    cum_m = jnp.concatenate([tril, select], axis=0).astype(BF16)
    col = lambda base: (lambda bi, gi, si: (bi, si, base * groups + gi))
    blocks = 4 * _nbytes((t, gw), u.dtype) + _nbytes((t, gw), BF16)
    resident = (2 * _nbytes(cum_m.shape, BF16) + _nbytes((heads_per_step, d, d), F32)
                + heads_per_step * (16 * _nbytes((t, d), F32) + 2 * _nbytes((t, blk), F32)))
    return pl.pallas_call(
        functools.partial(_hgrn_kernel, layer=layer, heads_per_step=heads_per_step),
        grid=(b, groups, s // t),
        in_specs=[
            pl.BlockSpec((1, t, gw), col(2)),
            pl.BlockSpec((1, t, gw), col(3)),
            pl.BlockSpec((1, t, gw), col(4)),
            pl.BlockSpec((1, t, gw), col(5)),
            pl.BlockSpec((lower_bounds.shape[0], gw), lambda bi, gi, si: (0, gi)),
            pl.BlockSpec((1, gw), lambda bi, gi, si: (0, gi)),
            pl.BlockSpec(cum_m.shape, lambda bi, gi, si: (0, 0)),
        ],
        out_specs=pl.BlockSpec((1, t, gw), lambda bi, gi, si: (bi, si, gi)),
        out_shape=jax.ShapeDtypeStruct((b, s, width), BF16),
        scratch_shapes=[pltpu.VMEM((heads_per_step, d, d), F32)],
        compiler_params=pltpu.CompilerParams(
            dimension_semantics=("arbitrary", "arbitrary", "arbitrary"),
            vmem_limit_bytes=_vmem_limit(blocks, resident)),
        name="hgrn2",
    )(u, u, u, u, lower_bounds, norm_g.reshape(1, width), cum_m)


def _sb_weights(tiles, later_m):
    nt_dims = (((1,), (1,)), ((), ()))
    z = [lax.dot_general(q, kb, nt_dims, preferred_element_type=F32) for q, kb, _, _ in tiles]
    log_beta, log_keep_b, carry = [], [], []
    for x, (_, _, c, mask) in zip(z, tiles):
        if mask is not None:
            x = jnp.where(mask, x, SB_MASKED_LOGIT)
        soft = jnp.log(1.0 + jnp.exp2(-jnp.abs(x))) * LOG2E
        lb = jnp.minimum(x, 0.0) - soft
        lk = lb - x
        log_beta.append(lb)
        log_keep_b.append(lk.astype(BF16))
        carry.append(c + jnp.sum(lk, axis=-1, keepdims=True))
    later = [jnp.dot(lk, later_m, preferred_element_type=F32) for lk in log_keep_b]
    a = [jnp.exp2(lb + lt + t[2]).astype(BF16) for lb, lt, t in zip(log_beta, later, tiles)]
    return list(zip(carry, a))


def _sb_kernel(q_ref, k_ref, v_ref, gate_ref, m_ref, o_ref, *, n_sub, tr):
    qi = pl.program_id(2)
    later_wide = m_ref[...]
    later_m = later_wide[:tr, :tr]
    col = lax.broadcasted_iota(jnp.int32, (tr, 2 * tr), 1)
    row = lax.broadcasted_iota(jnp.int32, (tr, 2 * tr), 0)
    causal_window = col - tr < row
    causal_start = col < row

    def rows(j, n):
        return pl.ds(pl.multiple_of(j * tr, tr), n * tr)

    def write(s, acc):
        out = slice(s * tr, (s + 1) * tr)
        o_ref[0, out, :] = (acc * _silu(gate_ref[0, out, :].astype(F32))).astype(o_ref.dtype)

    def step(sequence_start):
        qs = [q_ref[0, s * tr:(s + 1) * tr, :] for s in range(n_sub)]
        blk = [qi * n_sub + s for s in range(n_sub)]
        no_carry = jnp.zeros((tr, 1), F32)
        start = [0 if (sequence_start and s == 0) else blk[s] - 1 for s in range(n_sub)]
        first = _sb_weights(
            [(qs[s], k_ref[0, rows(start[s], 2), :], no_carry,
              causal_start if (sequence_start and s == 0) else causal_window)
             for s in range(n_sub)], later_wide)
        with_second = [s for s in range(n_sub) if not (sequence_start and s < 2)]
        second = dict(zip(with_second, _sb_weights(
            [(qs[s], k_ref[0, rows(blk[s] - 2, 1), :], first[s][0], None) for s in with_second],
            later_m)))
        state = []
        for s in range(n_sub):
            if s in second:
                carry, a_far = second[s]
                weights = jnp.concatenate([a_far, first[s][1]], axis=1)
                acc = jnp.dot(weights, v_ref[0, rows(blk[s] - 2, 3), :],
                              preferred_element_type=F32)
                nxt = blk[s] - 3
            else:
                carry, weights = first[s]
                acc = jnp.dot(weights, v_ref[0, rows(start[s], 2), :], preferred_element_type=F32)
                nxt = -1
            write(s, acc)
            state.append((qs[s], nxt, carry, acc))

        worst = functools.reduce(jnp.maximum, [c for _, _, c, _ in state])

        @pl.when(jnp.max(worst) > SB_DEAD_LOG2_WEIGHT)
        def _():
            for s, (q, nxt, carry, acc) in enumerate(state):
                def cond(st):
                    j, alive, _, _ = st
                    return jnp.logical_and(j >= 0, alive)

                def body(st, q=q):
                    j, _, carry, acc = st
                    ((carry, a),) = _sb_weights([(q, k_ref[0, rows(j, 1), :], carry, None)], later_m)
                    acc = acc + jnp.dot(a, v_ref[0, rows(j, 1), :], preferred_element_type=F32)
                    return j - 1, jnp.max(carry) > SB_DEAD_LOG2_WEIGHT, carry, acc

                alive = jnp.max(carry) > SB_DEAD_LOG2_WEIGHT
                _, _, _, acc = lax.while_loop(cond, body, (nxt, alive, carry, acc))
                write(s, acc)

    @pl.when(qi == 0)
    def _():
        step(True)

    @pl.when(qi > 0)
    def _():
        step(False)


def _stick_breaking(u, b, tq, tr):
    four_h, rows, d = u.shape
    heads = four_h // 4
    dm = heads * d
    s = rows // b
    nq = s // tq
    assert s >= 2 * tr and tq % tr == 0
    row = jnp.arange(2 * tr)
    later_m = (row[:, None] > row[None, :]).astype(BF16)
    blocks = 2 * _nbytes((s, d), BF16) + 3 * _nbytes((tq, d), BF16)
    resident = _nbytes(later_m.shape, BF16) + (tq // tr) * 12 * _nbytes((tr, 2 * tr), F32)
    return pl.pallas_call(
        functools.partial(_sb_kernel, n_sub=tq // tr, tr=tr),
        grid=(b, heads, nq),
        in_specs=[
            pl.BlockSpec((1, tq, d), lambda bi, hi, qi: (hi, bi * nq + qi, 0)),
            pl.BlockSpec((1, s, d), lambda bi, hi, qi: (heads + hi, bi, 0)),
            pl.BlockSpec((1, s, d), lambda bi, hi, qi: (2 * heads + hi, bi, 0)),
            pl.BlockSpec((1, tq, d), lambda bi, hi, qi: (3 * heads + hi, bi * nq + qi, 0)),
            pl.BlockSpec(later_m.shape, lambda bi, hi, qi: (0, 0)),
        ],
        out_specs=pl.BlockSpec((1, tq, d), lambda bi, hi, qi: (bi, qi, hi)),
        out_shape=jax.ShapeDtypeStruct((b, s, dm), BF16),
        compiler_params=pltpu.CompilerParams(
            dimension_semantics=("arbitrary", "arbitrary", "arbitrary"),
            vmem_limit_bytes=_vmem_limit(blocks, resident)),
        name="stick_breaking",
    )(u, u, u, u, later_m)


def _xattn_kernel(q_ref, k_ref, v_ref, o_ref, *, heads):
    dm = q_ref.shape[2]
    hd = dm // heads
    scale = 1.0 / math.sqrt(hd)
    cols = [slice(h * hd, (h + 1) * hd) for h in range(heads)]
    s = [lax.dot_general(q_ref[0, :, c], k_ref[0, :, c], (((1,), (1,)), ((), ())),
                         preferred_element_type=F32) * scale for c in cols]
    e = [jnp.exp(x - jnp.max(x, axis=-1, keepdims=True)) for x in s]
    p = [(x * (1.0 / jnp.sum(x, axis=-1, keepdims=True))).astype(BF16) for x in e]
    o = [jnp.dot(x, v_ref[0, :, c], preferred_element_type=F32) for x, c in zip(p, cols)]
    for x, c in zip(o, cols):
        o_ref[0, :, c] = x.astype(o_ref.dtype)


def _cross_attention_core(q, kv, t):
    b, s, dm = q.shape
    mlen = kv.shape[1]
    blocks = 2 * _nbytes((t, dm), BF16) + 2 * _nbytes((mlen, dm), BF16)
    resident = 6 * _nbytes((t, mlen), F32) + 2 * _nbytes((t, dm // XA_HEADS), F32)
    return pl.pallas_call(
        functools.partial(_xattn_kernel, heads=XA_HEADS),
        grid=(b, s // t),
        in_specs=[
            pl.BlockSpec((1, t, dm), lambda bi, si: (bi, si, 0)),
            pl.BlockSpec((1, mlen, dm), lambda bi, si: (bi, 0, 0)),
            pl.BlockSpec((1, mlen, dm), lambda bi, si: (bi, 0, 1)),
        ],
        out_specs=pl.BlockSpec((1, t, dm), lambda bi, si: (bi, si, 0)),
        out_shape=jax.ShapeDtypeStruct((b, s, dm), BF16),
        compiler_params=pltpu.CompilerParams(
            dimension_semantics=("arbitrary", "arbitrary"),
            vmem_limit_bytes=_vmem_limit(blocks, resident)),
        name="cross_attention",
    )(q, kv, kv)


MM_TM = 1024
MM_TN = 1024
MM_TN_BF16_OUT = 2048
RES_TM = 512
POOL_ROWS = 512
HGRN_ROWS = 512
HGRN_BLOCK = 256
HGRN_HEADS_PER_STEP = 4
SB_Q_ROWS = 1024
SB_K_ROWS = 128
XA_ROWS = 1024


def kernel(x, mem, norm_mix_g, norm_xa_g, norm_mem_g, final_norm_g, w_in_even, pool_w,
           pool_scale, hgrn_lower_bounds, hgrn_norm_g, w_out_even, w_in_odd, w_out_odd,
           xa_wq, xa_wkv, xa_wo):
    b, s, dm = x.shape
    mlen = mem.shape[1]
    depth = norm_mix_g.shape[0]
    rows = b * s
    xs = x.reshape(rows, dm)
    mem2 = mem.reshape(b * mlen, dm)
    sb_col_scale = jnp.concatenate([jnp.full((dm,), LOG2E / math.sqrt(SB_HEAD_DIM), F32),
                                    jnp.ones((3 * dm,), F32)])
    w_in_even_b, w_out_even_b = w_in_even.astype(BF16), w_out_even.astype(BF16)
    w_in_odd_b, w_out_odd_b = (w_in_odd * sb_col_scale).astype(BF16), w_out_odd.astype(BF16)
    wq_b, wo_b = xa_wq.astype(BF16), xa_wo.astype(BF16)
    h = None
    for l in range(depth):
        if l % 2 == 0:
            e = l // 2
            if h is None:
                u = _norm_matmul(xs, norm_mix_g[l], w_in_even_b, e, BF16, MM_TM, MM_TN_BF16_OUT)
            else:
                u = _matmul(h, w_in_even_b, e, BF16, MM_TM, MM_TN_BF16_OUT)
            u = u.reshape(b, s, -1)
            y_a = _pool_mixer(u, pool_w[e].astype(BF16), pool_scale[e], POOL_ROWS)
            y_b = _hgrn2(u, hgrn_lower_bounds, hgrn_norm_g[e], e, HGRN_ROWS,
                         HGRN_HEADS_PER_STEP)
            mixed = [y_a.reshape(rows, -1), y_b.reshape(rows, -1)]
            w_out, w_out_layer = w_out_even_b, e
        else:
            o = l // 2
            assert h is not None, "stick-breaking layers follow a layer that emits h"
            u = _matmul(h, w_in_odd_b, o, BF16, MM_TM, MM_TN_BF16_OUT, head_dim=SB_HEAD_DIM)
            mixed = [_stick_breaking(u, b, SB_Q_ROWS, SB_K_ROWS).reshape(rows, dm)]
            w_out, w_out_layer = w_out_odd_b, o
        xs, h_xa = _matmul_residual(mixed, w_out, w_out_layer, xs, norm_xa_g[l], RES_TM, "emit")
        q = _matmul(h_xa, wq_b, l, BF16, MM_TM, MM_TN_BF16_OUT)
        kv = _norm_matmul(mem2, norm_mem_g[l], xa_wkv, l, BF16, b * mlen, MM_TN)
        att = _cross_attention_core(q.reshape(b, s, dm), kv.reshape(b, mlen, 2 * dm), XA_ROWS)
        att = [att.reshape(rows, dm)]
        if l == depth - 1:
            (xs,) = _matmul_residual(att, wo_b, l, xs, final_norm_g, RES_TM, "final")
        else:
            xs, h = _matmul_residual(att, wo_b, l, xs, norm_mix_g[l + 1], RES_TM, "emit")
    return xs.reshape(b, s, dm)
```

```python
import functools
import math

import jax
import jax.numpy as jnp
from jax import lax
from jax.experimental import pallas as pl
from jax.experimental.pallas import tpu as pltpu

F32 = jnp.float32
BF16 = jnp.bfloat16

EPS = 1e-6
LOG2E = math.log2(math.e)
POOL_WINDOWS = (2, 4, 8, 16)
POOL_HALO = 32
F32_SUBLANES = 8
HGRN_HEAD_DIM = 128
HGRN_CHUNK = 32
BF16_SUBLANES = 16
SB_HEAD_DIM = 128
XA_HEADS = 4

V7X_VMEM_BYTES = 64 * 1024 * 1024
VMEM_CAP_BYTES = V7X_VMEM_BYTES - 8 * 1024 * 1024
COMPILER_SCRATCH_BYTES = 8 * 1024 * 1024

SB_DEAD_LOG2_WEIGHT = -151.0
SB_MASKED_LOGIT = -1e30


def _vmem_limit(pipelined_bytes, resident_bytes=0):
    need = 2 * pipelined_bytes + resident_bytes + COMPILER_SCRATCH_BYTES
    return int(min(VMEM_CAP_BYTES, need))


def _nbytes(shape, dtype):
    return math.prod(shape) * jnp.dtype(dtype).itemsize


def _sigmoid(x):
    return 0.5 * jnp.tanh(0.5 * x) + 0.5


def _silu(x):
    return x * _sigmoid(x)


def _split_bf16(x):
    hi = x.astype(BF16)
    lo = (x - hi.astype(F32)).astype(BF16)
    return hi, lo


def _rms_scale(x):
    return x * lax.rsqrt(jnp.mean(x * x, axis=-1, keepdims=True) + EPS)


NORM_ROWS = 256


def _norm_matmul_kernel(x_ref, g_ref, w_ref, o_ref, h_ref):
    tm = x_ref.shape[0]

    @pl.when(pl.program_id(1) == 0)
    def _():
        g = g_ref[...]
        w = w_ref[...].astype(h_ref.dtype)
        for r in range(0, tm, NORM_ROWS):
            rows = slice(r, r + NORM_ROWS)
            h = (_rms_scale(x_ref[rows, :]) * g).astype(h_ref.dtype)
            h_ref[rows, :] = h
            o_ref[rows, :] = jnp.dot(h, w, preferred_element_type=F32).astype(o_ref.dtype)

    @pl.when(pl.program_id(1) > 0)
    def _():
        o_ref[...] = jnp.dot(h_ref[...], w_ref[...].astype(h_ref.dtype),
                             preferred_element_type=F32).astype(o_ref.dtype)


def _norm_matmul(x, g, w, layer, out_dtype, tm, tn):
    m, k = x.shape
    n = w.shape[2]
    assert m % tm == 0 and n % tn == 0 and tm % NORM_ROWS == 0
    blocks = (_nbytes((tm, k), x.dtype) + _nbytes((k, tn), w.dtype)
              + _nbytes((tm, tn), out_dtype))
    scratch = _nbytes((tm, k), BF16) + _nbytes((tm, tn), F32)
    return pl.pallas_call(
        _norm_matmul_kernel,
        grid=(m // tm, n // tn),
        in_specs=[
            pl.BlockSpec((tm, k), lambda i, j: (i, 0)),
            pl.BlockSpec((1, k), lambda i, j: (0, 0)),
            pl.BlockSpec((None, k, tn), lambda i, j: (layer, 0, j)),
        ],
        out_specs=pl.BlockSpec((tm, tn), lambda i, j: (i, j)),
        out_shape=jax.ShapeDtypeStruct((m, n), out_dtype),
        scratch_shapes=[pltpu.VMEM((tm, k), BF16)],
        compiler_params=pltpu.CompilerParams(
            dimension_semantics=("arbitrary", "arbitrary"),
            vmem_limit_bytes=_vmem_limit(blocks, scratch)),
        name="norm_matmul",
    )(x, g.reshape(1, k), w)


def _matmul_kernel(a_ref, w_ref, o_ref):
    res = jnp.dot(a_ref[...], w_ref[...], preferred_element_type=F32).astype(o_ref.dtype)
    if len(o_ref.shape) == 2:
        o_ref[...] = res
    else:
        head_dim = o_ref.shape[2]
        for hh in range(o_ref.shape[0]):
            o_ref[hh] = res[:, hh * head_dim:(hh + 1) * head_dim]


def _matmul(a, w, layer, out_dtype, tm, tn, head_dim=None):
    m, k = a.shape
    n = w.shape[2]
    assert m % tm == 0 and n % tn == 0
    blocks = (_nbytes((tm, k), a.dtype) + _nbytes((k, tn), w.dtype)
              + _nbytes((tm, tn), out_dtype))
    if head_dim is None:
        out_spec = pl.BlockSpec((tm, tn), lambda i, j: (i, j))
        out_shape = jax.ShapeDtypeStruct((m, n), out_dtype)
    else:
        out_spec = pl.BlockSpec((tn // head_dim, tm, head_dim), lambda i, j: (j, i, 0))
        out_shape = jax.ShapeDtypeStruct((n // head_dim, m, head_dim), out_dtype)
    return pl.pallas_call(
        _matmul_kernel,
        grid=(m // tm, n // tn),
        in_specs=[pl.BlockSpec((tm, k), lambda i, j: (i, 0)),
                  pl.BlockSpec((None, k, tn), lambda i, j: (layer, 0, j))],
        out_specs=out_spec,
        out_shape=out_shape,
        compiler_params=pltpu.CompilerParams(
            dimension_semantics=("arbitrary", "arbitrary"),
            vmem_limit_bytes=_vmem_limit(blocks, _nbytes((tm, tn), F32))),
        name="matmul",
    )(a, w)


def _matmul_residual_kernel(*refs, n_lhs, norm_mode):
    lhs_refs = refs[:n_lhs]
    w_refs = refs[n_lhs:2 * n_lhs]
    res_ref, g_ref = refs[2 * n_lhs], refs[2 * n_lhs + 1]
    out_refs = refs[2 * n_lhs + 2:]
    acc = res_ref[...]
    for a_ref, w_ref in zip(lhs_refs, w_refs):
        acc = acc + jnp.dot(a_ref[...], w_ref[...], preferred_element_type=F32)
    normed = _rms_scale(acc) * g_ref[...]
    if norm_mode == "final":
        out_refs[0][...] = normed
    else:
        out_refs[0][...] = acc
        out_refs[1][...] = normed.astype(out_refs[1].dtype)


def _matmul_residual(lhs_list, w, layer, res, g, tm, norm_mode):
    assert norm_mode in ("emit", "final")
    m, n = res.shape
    kp = lhs_list[0].shape[1]
    assert all(a.shape == (m, kp) for a in lhs_list)
    assert w.shape[1:] == (kp * len(lhs_list), n)
    n_lhs = len(lhs_list)
    row_block = pl.BlockSpec((tm, n), lambda i: (i, 0))
    in_specs = [pl.BlockSpec((tm, kp), lambda i: (i, 0)) for _ in lhs_list]
    in_specs += [pl.BlockSpec((None, kp, n), functools.partial(lambda i, p: (layer, p, 0), p=p))
                 for p in range(n_lhs)]
    in_specs += [row_block, pl.BlockSpec((1, n), lambda i: (0, 0))]
    out_shape = [jax.ShapeDtypeStruct((m, n), F32)]
    if norm_mode == "emit":
        out_shape += [jax.ShapeDtypeStruct((m, n), BF16)]
    blocks = (n_lhs * (_nbytes((tm, kp), BF16) + _nbytes((kp, n), BF16))
              + 2 * _nbytes((tm, n), F32) + _nbytes((tm, n), BF16))
    return pl.pallas_call(
        functools.partial(_matmul_residual_kernel, n_lhs=n_lhs, norm_mode=norm_mode),
        grid=(m // tm,),
        in_specs=in_specs,
        out_specs=[row_block] * len(out_shape),
        out_shape=out_shape,
        compiler_params=pltpu.CompilerParams(
            dimension_semantics=("arbitrary",),
            vmem_limit_bytes=_vmem_limit(blocks, 2 * _nbytes((tm, n), F32))),
        name="matmul_residual",
    )(*lhs_list, *([w] * n_lhs), res, g.reshape(1, n))


def _pool_kernel(a_ref, gate_ref, w_ref, scale_ref, o_ref, buf_ref, part_a, part_b, *, group):
    t = a_ref.shape[1]
    end = POOL_HALO + t
    sblk = pl.program_id(1)

    @pl.when(sblk == 0)
    def _():
        buf_ref[0:POOL_HALO, :] = jnp.zeros((POOL_HALO, buf_ref.shape[1]), F32)

    buf_ref[POOL_HALO:end, :] = a_ref[0].astype(F32)
    pos = sblk * t + lax.broadcasted_iota(jnp.int32, (t, 1), 0)
    for gi, win in enumerate(POOL_WINDOWS):
        cols = slice(gi * group, (gi + 1) * group)
        cur = buf_ref[POOL_HALO:end, cols]
        levels = win.bit_length() - 1
        load = lambda lo, hi, cols=cols: buf_ref[lo:hi, cols]
        for k in range(1, levels + 1):
            shift = 2 ** (k - 1)
            lo = POOL_HALO - F32_SUBLANES * (levels - k)
            tot = load(lo, end) + load(lo - shift, end - shift)
            if k < levels:
                part = part_a if k % 2 else part_b
                part[lo:end, :] = tot
                load = lambda lo, hi, part=part: part[lo:hi, :]
        cnt = jnp.minimum(pos + 1, win).astype(F32)
        mixed = tot / cnt - cur
        y = jnp.dot(mixed.astype(BF16), w_ref[gi], preferred_element_type=F32)
        y = y * scale_ref[:, cols]
        o_ref[0, :, cols] = (y * _silu(gate_ref[0, :, cols].astype(F32))).astype(o_ref.dtype)
    buf_ref[0:POOL_HALO, :] = buf_ref[t:t + POOL_HALO, :]


def _pool_mixer(u, pool_w, pool_scale, t):
    b, s, _ = u.shape
    n_groups, group, _ = pool_w.shape
    width = n_groups * group
    blocks = 2 * _nbytes((t, width), u.dtype) + _nbytes((t, width), BF16)
    assert POOL_HALO >= F32_SUBLANES * (max(POOL_WINDOWS).bit_length() - 1)
    resident = (_nbytes((t + POOL_HALO, width), F32) + _nbytes(pool_w.shape, BF16)
                + 6 * _nbytes((t + POOL_HALO, group), F32))
    return pl.pallas_call(
        functools.partial(_pool_kernel, group=group),
        grid=(b, s // t),
        in_specs=[
            pl.BlockSpec((1, t, width), lambda bi, si: (bi, si, 0)),
            pl.BlockSpec((1, t, width), lambda bi, si: (bi, si, 1)),
            pl.BlockSpec(pool_w.shape, lambda bi, si: (0, 0, 0)),
            pl.BlockSpec((1, width), lambda bi, si: (0, 0)),
        ],
        out_specs=pl.BlockSpec((1, t, width), lambda bi, si: (bi, si, 0)),
        out_shape=jax.ShapeDtypeStruct((b, s, width), BF16),
        scratch_shapes=[pltpu.VMEM((t + POOL_HALO, width), F32),
                        pltpu.VMEM((t + POOL_HALO, group), F32),
                        pltpu.VMEM((t + POOL_HALO, group), F32)],
        compiler_params=pltpu.CompilerParams(
            dimension_semantics=("arbitrary", "arbitrary"),
            vmem_limit_bytes=_vmem_limit(blocks, resident)),
        name="pool_mixer",
    )(u, u, pool_w, pool_scale.reshape(1, width))


def _hgrn_kernel(q_ref, f_ref, i_ref, gate_ref, lbp_ref, ng_ref, cum_ref,
                 o_ref, state_ref, *, layer, heads_per_step):
    t = q_ref.shape[1]
    d = HGRN_HEAD_DIM
    gw = heads_per_step * d
    n_chunks = t // HGRN_CHUNK
    nt_dims = (((1,), (1,)), ((), ()))
    tn_dims = (((0,), (0,)), ((), ()))

    @pl.when(pl.program_id(2) == 0)
    def _():
        state_ref[...] = jnp.zeros(state_ref.shape, F32)

    lbp = lbp_ref[...]
    ex = jnp.exp(lbp - jnp.max(lbp, axis=0, keepdims=True))
    lb = jnp.sum(ex[:layer + 1], axis=0, keepdims=True) / jnp.sum(ex, axis=0, keepdims=True)

    f = lb + (1.0 - lb) * _sigmoid(f_ref[0].astype(F32))
    logf = jnp.log(f)
    k = 1.0 - f

    cum_m = cum_ref[...]
    blk = cum_m.shape[1]
    blk_chunks = blk // HGRN_CHUNK
    blocks = [slice(r, r + blk) for r in range(0, t, blk)]
    hi, lo = _split_bf16(logf)
    hilo = jnp.concatenate([hi, lo], axis=1)
    sums = [jnp.dot(cum_m, hilo[r], preferred_element_type=F32) for r in blocks]
    sums = [x[:, :gw] + x[:, gw:] for x in sums]
    bcum = jnp.concatenate([x[:blk] for x in sums], axis=0)
    decay = jnp.exp(jnp.concatenate([x[blk:blk + blk_chunks] for x in sums], axis=0))

    q_t = (q_ref[0].astype(F32) * jnp.exp(bcum)).astype(BF16)
    k_t = k * jnp.exp(-bcum)
    k_t_b = k_t.astype(BF16)
    inp_b = i_ref[0].astype(BF16)
    in_chunk = cum_m[:blk] > 0
    heads = [slice(h * d, (h + 1) * d) for h in range(heads_per_step)]
    chunks = [slice(n * HGRN_CHUNK, (n + 1) * HGRN_CHUNK) for n in range(n_chunks)]

    scores = [[lax.dot_general(q_t[r, c], k_t_b[r, c], nt_dims, preferred_element_type=F32)
               for r in blocks] for c in heads]
    scores = [[jnp.where(in_chunk, s, 0.0).astype(BF16) for s in per_head]
              for per_head in scores]
    o_intra = [jnp.concatenate([jnp.dot(s, inp_b[r, c], preferred_element_type=F32)
                                for s, r in zip(per_head, blocks)], axis=0)
               for per_head, c in zip(scores, heads)]

    k_end = [(k_t[r] * decay[n:n + 1, :]).astype(BF16) for n, r in enumerate(chunks)]
    incr = [[lax.dot_general(inp_b[r, c], k_end[n][:, c], tn_dims, preferred_element_type=F32)
             for c in heads] for n, r in enumerate(chunks)]
    states = []
    cur = [state_ref[h] for h in range(heads_per_step)]
    for n in range(n_chunks):
        states.append([s.astype(BF16) for s in cur])
        cur = [s * decay[n:n + 1, c] + u for s, c, u in zip(cur, heads, incr[n])]
    for h in range(heads_per_step):
        state_ref[h] = cur[h]

    o_inter = [jnp.concatenate(
        [lax.dot_general(q_t[r, c], states[n][h], nt_dims, preferred_element_type=F32)
         for n, r in enumerate(chunks)], axis=0) for h, c in enumerate(heads)]

    for h, c in enumerate(heads):
        o = _rms_scale(o_intra[h] + o_inter[h]) * ng_ref[:, c]
        o_ref[0, :, c] = (o * _silu(gate_ref[0, :, c].astype(F32))).astype(o_ref.dtype)


def _hgrn2(u, lower_bounds, norm_g, layer, t, heads_per_step):
    b, s, six_w = u.shape
    width = six_w // 6
    d = HGRN_HEAD_DIM
    gw = heads_per_step * d
    groups = width // gw
    blk = HGRN_BLOCK
    assert t % blk == 0 and blk // HGRN_CHUNK <= BF16_SUBLANES
    pos = jnp.arange(blk, dtype=jnp.int32)
    chunk_id = pos // HGRN_CHUNK
    tril = (chunk_id[:, None] == chunk_id[None, :]) & (pos[None, :] <= pos[:, None])
    select = jnp.arange(BF16_SUBLANES, dtype=jnp.int32)[:, None] == chunk_id[None, :]
    cum_m = jnp.concatenate([tril, select], axis=0).astype(BF16)
    col = lambda base: (lambda bi, gi, si: (bi, si, base * groups + gi))
    blocks = 4 * _nbytes((t, gw), u.dtype) + _nbytes((t, gw), BF16)
    resident = (2 * _nbytes(cum_m.shape, BF16) + _nbytes((heads_per_step, d, d), F32)
                + heads_per_step * (16 * _nbytes((t, d), F32) + 2 * _nbytes((t, blk), F32)))
    return pl.pallas_call(
        functools.partial(_hgrn_kernel, layer=layer, heads_per_step=heads_per_step),
        grid=(b, groups, s // t),
        in_specs=[
            pl.BlockSpec((1, t, gw), col(2)),
            pl.BlockSpec((1, t, gw), col(3)),
            pl.BlockSpec((1, t, gw), col(4)),
            pl.BlockSpec((1, t, gw), col(5)),
            pl.BlockSpec((lower_bounds.shape[0], gw), lambda bi, gi, si: (0, gi)),
            pl.BlockSpec((1, gw), lambda bi, gi, si: (0, gi)),
            pl.BlockSpec(cum_m.shape, lambda bi, gi, si: (0, 0)),
        ],
        out_specs=pl.BlockSpec((1, t, gw), lambda bi, gi, si: (bi, si, gi)),
        out_shape=jax.ShapeDtypeStruct((b, s, width), BF16),
        scratch_shapes=[pltpu.VMEM((heads_per_step, d, d), F32)],
        compiler_params=pltpu.CompilerParams(
            dimension_semantics=("arbitrary", "arbitrary", "arbitrary"),
            vmem_limit_bytes=_vmem_limit(blocks, resident)),
        name="hgrn2",
    )(u, u, u, u, lower_bounds, norm_g.reshape(1, width), cum_m)


def _sb_weights(tiles, later_m):
    nt_dims = (((1,), (1,)), ((), ()))
    z = [lax.dot_general(q, kb, nt_dims, preferred_element_type=F32) for q, kb, _, _ in tiles]
    log_beta, log_keep_b, carry = [], [], []
    for x, (_, _, c, mask) in zip(z, tiles):
        if mask is not None:
            x = jnp.where(mask, x, SB_MASKED_LOGIT)
        soft = jnp.log(1.0 + jnp.exp2(-jnp.abs(x))) * LOG2E
        lb = jnp.minimum(x, 0.0) - soft
        lk = lb - x
        log_beta.append(lb)
        log_keep_b.append(lk.astype(BF16))
        carry.append(c + jnp.sum(lk, axis=-1, keepdims=True))
    later = [jnp.dot(lk, later_m, preferred_element_type=F32) for lk in log_keep_b]
    a = [jnp.exp2(lb + lt + t[2]).astype(BF16) for lb, lt, t in zip(log_beta, later, tiles)]
    return list(zip(carry, a))


def _sb_kernel(q_ref, k_ref, v_ref, gate_ref, m_ref, o_ref, *, n_sub, tr):
    qi = pl.program_id(2)
    later_wide = m_ref[...]
    later_m = later_wide[:tr, :tr]
    col = lax.broadcasted_iota(jnp.int32, (tr, 2 * tr), 1)
    row = lax.broadcasted_iota(jnp.int32, (tr, 2 * tr), 0)
    causal_window = col - tr < row
    causal_start = col < row

    def rows(j, n):
        return pl.ds(pl.multiple_of(j * tr, tr), n * tr)

    def write(s, acc):
        out = slice(s * tr, (s + 1) * tr)
        o_ref[0, out, :] = (acc * _silu(gate_ref[0, out, :].astype(F32))).astype(o_ref.dtype)

    def step(sequence_start):
        qs = [q_ref[0, s * tr:(s + 1) * tr, :] for s in range(n_sub)]
        blk = [qi * n_sub + s for s in range(n_sub)]
        no_carry = jnp.zeros((tr, 1), F32)
        start = [0 if (sequence_start and s == 0) else blk[s] - 1 for s in range(n_sub)]
        first = _sb_weights(
            [(qs[s], k_ref[0, rows(start[s], 2), :], no_carry,
              causal_start if (sequence_start and s == 0) else causal_window)
             for s in range(n_sub)], later_wide)
        with_second = [s for s in range(n_sub) if not (sequence_start and s < 2)]
        second = dict(zip(with_second, _sb_weights(
            [(qs[s], k_ref[0, rows(blk[s] - 2, 1), :], first[s][0], None) for s in with_second],
            later_m)))
        state = []
        for s in range(n_sub):
            if s in second:
                carry, a_far = second[s]
                weights = jnp.concatenate([a_far, first[s][1]], axis=1)
                acc = jnp.dot(weights, v_ref[0, rows(blk[s] - 2, 3), :],
                              preferred_element_type=F32)
                nxt = blk[s] - 3
            else:
                carry, weights = first[s]
                acc = jnp.dot(weights, v_ref[0, rows(start[s], 2), :], preferred_element_type=F32)
                nxt = -1
            write(s, acc)
            state.append((qs[s], nxt, carry, acc))

        worst = functools.reduce(jnp.maximum, [c for _, _, c, _ in state])

        @pl.when(jnp.max(worst) > SB_DEAD_LOG2_WEIGHT)
        def _():
            for s, (q, nxt, carry, acc) in enumerate(state):
                def cond(st):
                    j, alive, _, _ = st
                    return jnp.logical_and(j >= 0, alive)

                def body(st, q=q):
                    j, _, carry, acc = st
                    ((carry, a),) = _sb_weights([(q, k_ref[0, rows(j, 1), :], carry, None)], later_m)
                    acc = acc + jnp.dot(a, v_ref[0, rows(j, 1), :], preferred_element_type=F32)
                    return j - 1, jnp.max(carry) > SB_DEAD_LOG2_WEIGHT, carry, acc

                alive = jnp.max(carry) > SB_DEAD_LOG2_WEIGHT
                _, _, _, acc = lax.while_loop(cond, body, (nxt, alive, carry, acc))
                write(s, acc)

    @pl.when(qi == 0)
    def _():
        step(True)

    @pl.when(qi > 0)
    def _():
        step(False)


def _stick_breaking(u, b, tq, tr):
    four_h, rows, d = u.shape
    heads = four_h // 4
    dm = heads * d
    s = rows // b
    nq = s // tq
    assert s >= 2 * tr and tq % tr == 0
    row = jnp.arange(2 * tr)
    later_m = (row[:, None] > row[None, :]).astype(BF16)
    blocks = 2 * _nbytes((s, d), BF16) + 3 * _nbytes((tq, d), BF16)
    resident = _nbytes(later_m.shape, BF16) + (tq // tr) * 12 * _nbytes((tr, 2 * tr), F32)
    return pl.pallas_call(
        functools.partial(_sb_kernel, n_sub=tq // tr, tr=tr),
        grid=(b, heads, nq),
        in_specs=[
            pl.BlockSpec((1, tq, d), lambda bi, hi, qi: (hi, bi * nq + qi, 0)),
            pl.BlockSpec((1, s, d), lambda bi, hi, qi: (heads + hi, bi, 0)),
            pl.BlockSpec((1, s, d), lambda bi, hi, qi: (2 * heads + hi, bi, 0)),
            pl.BlockSpec((1, tq, d), lambda bi, hi, qi: (3 * heads + hi, bi * nq + qi, 0)),
            pl.BlockSpec(later_m.shape, lambda bi, hi, qi: (0, 0)),
        ],
        out_specs=pl.BlockSpec((1, tq, d), lambda bi, hi, qi: (bi, qi, hi)),
        out_shape=jax.ShapeDtypeStruct((b, s, dm), BF16),
        compiler_params=pltpu.CompilerParams(
            dimension_semantics=("arbitrary", "arbitrary", "arbitrary"),
            vmem_limit_bytes=_vmem_limit(blocks, resident)),
        name="stick_breaking",
    )(u, u, u, u, later_m)


def _xattn_kernel(q_ref, k_ref, v_ref, o_ref, *, heads):
    dm = q_ref.shape[2]
    hd = dm // heads
    scale = 1.0 / math.sqrt(hd)
    cols = [slice(h * hd, (h + 1) * hd) for h in range(heads)]
    s = [lax.dot_general(q_ref[0, :, c], k_ref[0, :, c], (((1,), (1,)), ((), ())),
                         preferred_element_type=F32) * scale for c in cols]
    e = [jnp.exp(x - jnp.max(x, axis=-1, keepdims=True)) for x in s]
    p = [(x * (1.0 / jnp.sum(x, axis=-1, keepdims=True))).astype(BF16) for x in e]
    o = [jnp.dot(x, v_ref[0, :, c], preferred_element_type=F32) for x, c in zip(p, cols)]
    for x, c in zip(o, cols):
        o_ref[0, :, c] = x.astype(o_ref.dtype)


def _cross_attention_core(q, kv, t):
    b, s, dm = q.shape
    mlen = kv.shape[1]
    blocks = 2 * _nbytes((t, dm), BF16) + 2 * _nbytes((mlen, dm), BF16)
    resident = 6 * _nbytes((t, mlen), F32) + 2 * _nbytes((t, dm // XA_HEADS), F32)
    return pl.pallas_call(
        functools.partial(_xattn_kernel, heads=XA_HEADS),
        grid=(b, s // t),
        in_specs=[
            pl.BlockSpec((1, t, dm), lambda bi, si: (bi, si, 0)),
            pl.BlockSpec((1, mlen, dm), lambda bi, si: (bi, 0, 0)),
            pl.BlockSpec((1, mlen, dm), lambda bi, si: (bi, 0, 1)),
        ],
        out_specs=pl.BlockSpec((1, t, dm), lambda bi, si: (bi, si, 0)),
        out_shape=jax.ShapeDtypeStruct((b, s, dm), BF16),
        compiler_params=pltpu.CompilerParams(
            dimension_semantics=("arbitrary", "arbitrary"),
            vmem_limit_bytes=_vmem_limit(blocks, resident)),
        name="cross_attention",
    )(q, kv, kv)


MM_TM = 1024
MM_TN = 1024
MM_TN_BF16_OUT = 2048
RES_TM = 512
POOL_ROWS = 1024
HGRN_ROWS = 512
HGRN_BLOCK = 256
HGRN_HEADS_PER_STEP = 4
SB_Q_ROWS = 2048
SB_K_ROWS = 128
XA_ROWS = 1024


def kernel(x, mem, norm_mix_g, norm_xa_g, norm_mem_g, final_norm_g, w_in_even, pool_w,
           pool_scale, hgrn_lower_bounds, hgrn_norm_g, w_out_even, w_in_odd, w_out_odd,
           xa_wq, xa_wkv, xa_wo):
    b, s, dm = x.shape
    mlen = mem.shape[1]
    depth = norm_mix_g.shape[0]
    rows = b * s
    xs = x.reshape(rows, dm)
    mem2 = mem.reshape(b * mlen, dm)
    sb_col_scale = jnp.concatenate([jnp.full((dm,), LOG2E / math.sqrt(SB_HEAD_DIM), F32),
                                    jnp.ones((3 * dm,), F32)])
    w_in_even_b, w_out_even_b = w_in_even.astype(BF16), w_out_even.astype(BF16)
    w_in_odd_b, w_out_odd_b = (w_in_odd * sb_col_scale).astype(BF16), w_out_odd.astype(BF16)
    wq_b, wo_b = xa_wq.astype(BF16), xa_wo.astype(BF16)
    h = None
    for l in range(depth):
        if l % 2 == 0:
            e = l // 2
            if h is None:
                u = _norm_matmul(xs, norm_mix_g[l], w_in_even_b, e, BF16, MM_TM, MM_TN_BF16_OUT)
            else:
                u = _matmul(h, w_in_even_b, e, BF16, MM_TM, MM_TN_BF16_OUT)
            u = u.reshape(b, s, -1)
            y_a = _pool_mixer(u, pool_w[e].astype(BF16), pool_scale[e], POOL_ROWS)
            y_b = _hgrn2(u, hgrn_lower_bounds, hgrn_norm_g[e], e, HGRN_ROWS,
                         HGRN_HEADS_PER_STEP)
            mixed = [y_a.reshape(rows, -1), y_b.reshape(rows, -1)]
            w_out, w_out_layer = w_out_even_b, e
        else:
            o = l // 2
            assert h is not None, "stick-breaking layers follow a layer that emits h"
            u = _matmul(h, w_in_odd_b, o, BF16, MM_TM, MM_TN_BF16_OUT, head_dim=SB_HEAD_DIM)
            mixed = [_stick_breaking(u, b, SB_Q_ROWS, SB_K_ROWS).reshape(rows, dm)]
            w_out, w_out_layer = w_out_odd_b, o
        xs, h_xa = _matmul_residual(mixed, w_out, w_out_layer, xs, norm_xa_g[l], RES_TM, "emit")
        q = _matmul(h_xa, wq_b, l, BF16, MM_TM, MM_TN_BF16_OUT)
        kv = _norm_matmul(mem2, norm_mem_g[l], xa_wkv, l, BF16, b * mlen, MM_TN)
        att = _cross_attention_core(q.reshape(b, s, dm), kv.reshape(b, mlen, 2 * dm), XA_ROWS)
        att = [att.reshape(rows, dm)]
        if l == depth - 1:
            (xs,) = _matmul_residual(att, wo_b, l, xs, final_norm_g, RES_TM, "final")
        else:
            xs, h = _matmul_residual(att, wo_b, l, xs, norm_mix_g[l + 1], RES_TM, "emit")
    return xs.reshape(b, s, dm)
```

```python
import functools
import math

import jax
import jax.numpy as jnp
from jax import lax
from jax.experimental import pallas as pl
from jax.experimental.pallas import tpu as pltpu

F32 = jnp.float32
BF16 = jnp.bfloat16

EPS = 1e-6
LOG2E = math.log2(math.e)
POOL_WINDOWS = (2, 4, 8, 16)
POOL_HALO = 32
F32_SUBLANES = 8
HGRN_HEAD_DIM = 128
HGRN_CHUNK = 32
BF16_SUBLANES = 16
SB_HEAD_DIM = 128
XA_HEADS = 4

V7X_VMEM_BYTES = 64 * 1024 * 1024
VMEM_CAP_BYTES = V7X_VMEM_BYTES - 8 * 1024 * 1024
COMPILER_SCRATCH_BYTES = 8 * 1024 * 1024

SB_DEAD_LOG2_WEIGHT = -151.0
SB_MASKED_LOGIT = -1e30


def _vmem_limit(pipelined_bytes, resident_bytes=0):
    need = 2 * pipelined_bytes + resident_bytes + COMPILER_SCRATCH_BYTES
    return int(min(VMEM_CAP_BYTES, need))


def _nbytes(shape, dtype):
    return math.prod(shape) * jnp.dtype(dtype).itemsize


def _sigmoid(x):
    return 0.5 * jnp.tanh(0.5 * x) + 0.5


def _silu(x):
    return x * _sigmoid(x)


def _split_bf16(x):
    hi = x.astype(BF16)
    lo = (x - hi.astype(F32)).astype(BF16)
    return hi, lo


def _rms_scale(x):
    return x * lax.rsqrt(jnp.mean(x * x, axis=-1, keepdims=True) + EPS)


NORM_ROWS = 256


def _norm_matmul_kernel(x_ref, g_ref, w_ref, o_ref, h_ref):
    tm = x_ref.shape[0]

    @pl.when(pl.program_id(1) == 0)
    def _():
        g = g_ref[...]
        w = w_ref[...].astype(h_ref.dtype)
        for r in range(0, tm, NORM_ROWS):
            rows = slice(r, r + NORM_ROWS)
            h = (_rms_scale(x_ref[rows, :]) * g).astype(h_ref.dtype)
            h_ref[rows, :] = h
            o_ref[rows, :] = jnp.dot(h, w, preferred_element_type=F32).astype(o_ref.dtype)

    @pl.when(pl.program_id(1) > 0)
    def _():
        o_ref[...] = jnp.dot(h_ref[...], w_ref[...].astype(h_ref.dtype),
                             preferred_element_type=F32).astype(o_ref.dtype)


def _norm_matmul(x, g, w, layer, out_dtype, tm, tn):
    m, k = x.shape
    n = w.shape[2]
    assert m % tm == 0 and n % tn == 0 and tm % NORM_ROWS == 0
    blocks = (_nbytes((tm, k), x.dtype) + _nbytes((k, tn), w.dtype)
              + _nbytes((tm, tn), out_dtype))
    scratch = _nbytes((tm, k), BF16) + _nbytes((tm, tn), F32)
    return pl.pallas_call(
        _norm_matmul_kernel,
        grid=(m // tm, n // tn),
        in_specs=[
            pl.BlockSpec((tm, k), lambda i, j: (i, 0)),
            pl.BlockSpec((1, k), lambda i, j: (0, 0)),
            pl.BlockSpec((None, k, tn), lambda i, j: (layer, 0, j)),
        ],
        out_specs=pl.BlockSpec((tm, tn), lambda i, j: (i, j)),
        out_shape=jax.ShapeDtypeStruct((m, n), out_dtype),
        scratch_shapes=[pltpu.VMEM((tm, k), BF16)],
        compiler_params=pltpu.CompilerParams(
            dimension_semantics=("arbitrary", "arbitrary"),
            vmem_limit_bytes=_vmem_limit(blocks, scratch)),
        name="norm_matmul",
    )(x, g.reshape(1, k), w)


def _matmul_kernel(a_ref, w_ref, o_ref):
    res = jnp.dot(a_ref[...], w_ref[...], preferred_element_type=F32).astype(o_ref.dtype)
    if len(o_ref.shape) == 2:
        o_ref[...] = res
    else:
        head_dim = o_ref.shape[2]
        for hh in range(o_ref.shape[0]):
            o_ref[hh] = res[:, hh * head_dim:(hh + 1) * head_dim]


def _matmul(a, w, layer, out_dtype, tm, tn, head_dim=None):
    m, k = a.shape
    n = w.shape[2]
    assert m % tm == 0 and n % tn == 0
    blocks = (_nbytes((tm, k), a.dtype) + _nbytes((k, tn), w.dtype)
              + _nbytes((tm, tn), out_dtype))
    if head_dim is None:
        out_spec = pl.BlockSpec((tm, tn), lambda i, j: (i, j))
        out_shape = jax.ShapeDtypeStruct((m, n), out_dtype)
    else:
        out_spec = pl.BlockSpec((tn // head_dim, tm, head_dim), lambda i, j: (j, i, 0))
        out_shape = jax.ShapeDtypeStruct((n // head_dim, m, head_dim), out_dtype)
    return pl.pallas_call(
        _matmul_kernel,
        grid=(m // tm, n // tn),
        in_specs=[pl.BlockSpec((tm, k), lambda i, j: (i, 0)),
                  pl.BlockSpec((None, k, tn), lambda i, j: (layer, 0, j))],
        out_specs=out_spec,
        out_shape=out_shape,
        compiler_params=pltpu.CompilerParams(
            dimension_semantics=("arbitrary", "arbitrary"),
            vmem_limit_bytes=_vmem_limit(blocks, _nbytes((tm, tn), F32))),
        name="matmul",
    )(a, w)


def _matmul_residual_kernel(*refs, n_lhs, norm_mode):
    lhs_refs = refs[:n_lhs]
    w_refs = refs[n_lhs:2 * n_lhs]
    res_ref, g_ref = refs[2 * n_lhs], refs[2 * n_lhs + 1]
    out_refs = refs[2 * n_lhs + 2:]
    acc = res_ref[...]
    for a_ref, w_ref in zip(lhs_refs, w_refs):
        acc = acc + jnp.dot(a_ref[...], w_ref[...], preferred_element_type=F32)
    normed = _rms_scale(acc) * g_ref[...]
    if norm_mode == "final":
        out_refs[0][...] = normed
    else:
        out_refs[0][...] = acc
        out_refs[1][...] = normed.astype(out_refs[1].dtype)


def _matmul_residual(lhs_list, w, layer, res, g, tm, norm_mode):
    assert norm_mode in ("emit", "final")
    m, n = res.shape
    kp = lhs_list[0].shape[1]
    assert all(a.shape == (m, kp) for a in lhs_list)
    assert w.shape[1:] == (kp * len(lhs_list), n)
    n_lhs = len(lhs_list)
    row_block = pl.BlockSpec((tm, n), lambda i: (i, 0))
    in_specs = [pl.BlockSpec((tm, kp), lambda i: (i, 0)) for _ in lhs_list]
    in_specs += [pl.BlockSpec((None, kp, n), functools.partial(lambda i, p: (layer, p, 0), p=p))
                 for p in range(n_lhs)]
    in_specs += [row_block, pl.BlockSpec((1, n), lambda i: (0, 0))]
    out_shape = [jax.ShapeDtypeStruct((m, n), F32)]
    if norm_mode == "emit":
        out_shape += [jax.ShapeDtypeStruct((m, n), BF16)]
    blocks = (n_lhs * (_nbytes((tm, kp), BF16) + _nbytes((kp, n), BF16))
              + 2 * _nbytes((tm, n), F32) + _nbytes((tm, n), BF16))
    return pl.pallas_call(
        functools.partial(_matmul_residual_kernel, n_lhs=n_lhs, norm_mode=norm_mode),
        grid=(m // tm,),
        in_specs=in_specs,
        out_specs=[row_block] * len(out_shape),
        out_shape=out_shape,
        compiler_params=pltpu.CompilerParams(
            dimension_semantics=("arbitrary",),
            vmem_limit_bytes=_vmem_limit(blocks, 2 * _nbytes((tm, n), F32))),
        name="matmul_residual",
    )(*lhs_list, *([w] * n_lhs), res, g.reshape(1, n))


def _pool_kernel(a_ref, gate_ref, w_ref, scale_ref, o_ref, buf_ref, part_a, part_b, *, group):
    t = a_ref.shape[1]
    end = POOL_HALO + t
    sblk = pl.program_id(1)

    @pl.when(sblk == 0)
    def _():
        buf_ref[0:POOL_HALO, :] = jnp.zeros((POOL_HALO, buf_ref.shape[1]), F32)

    buf_ref[POOL_HALO:end, :] = a_ref[0].astype(F32)
    pos = sblk * t + lax.broadcasted_iota(jnp.int32, (t, 1), 0)
    for gi, win in enumerate(POOL_WINDOWS):
        cols = slice(gi * group, (gi + 1) * group)
        cur = buf_ref[POOL_HALO:end, cols]
        levels = win.bit_length() - 1
        load = lambda lo, hi, cols=cols: buf_ref[lo:hi, cols]
        for k in range(1, levels + 1):
            shift = 2 ** (k - 1)
            lo = POOL_HALO - F32_SUBLANES * (levels - k)
            tot = load(lo, end) + load(lo - shift, end - shift)
            if k < levels:
                part = part_a if k % 2 else part_b
                part[lo:end, :] = tot
                load = lambda lo, hi, part=part: part[lo:hi, :]
        cnt = jnp.minimum(pos + 1, win).astype(F32)
        mixed = tot / cnt - cur
        y = jnp.dot(mixed.astype(BF16), w_ref[gi], preferred_element_type=F32)
        y = y * scale_ref[:, cols]
        o_ref[0, :, cols] = (y * _silu(gate_ref[0, :, cols].astype(F32))).astype(o_ref.dtype)
    buf_ref[0:POOL_HALO, :] = buf_ref[t:t + POOL_HALO, :]


def _pool_mixer(u, pool_w, pool_scale, t):
    b, s, _ = u.shape
    n_groups, group, _ = pool_w.shape
    width = n_groups * group
    blocks = 2 * _nbytes((t, width), u.dtype) + _nbytes((t, width), BF16)
    assert POOL_HALO >= F32_SUBLANES * (max(POOL_WINDOWS).bit_length() - 1)
    resident = (_nbytes((t + POOL_HALO, width), F32) + _nbytes(pool_w.shape, BF16)
                + 6 * _nbytes((t + POOL_HALO, group), F32))
    return pl.pallas_call(
        functools.partial(_pool_kernel, group=group),
        grid=(b, s // t),
        in_specs=[
            pl.BlockSpec((1, t, width), lambda bi, si: (bi, si, 0)),
            pl.BlockSpec((1, t, width), lambda bi, si: (bi, si, 1)),
            pl.BlockSpec(pool_w.shape, lambda bi, si: (0, 0, 0)),
            pl.BlockSpec((1, width), lambda bi, si: (0, 0)),
        ],
        out_specs=pl.BlockSpec((1, t, width), lambda bi, si: (bi, si, 0)),
        out_shape=jax.ShapeDtypeStruct((b, s, width), BF16),
        scratch_shapes=[pltpu.VMEM((t + POOL_HALO, width), F32),
                        pltpu.VMEM((t + POOL_HALO, group), F32),
                        pltpu.VMEM((t + POOL_HALO, group), F32)],
        compiler_params=pltpu.CompilerParams(
            dimension_semantics=("arbitrary", "arbitrary"),
            vmem_limit_bytes=_vmem_limit(blocks, resident)),
        name="pool_mixer",
    )(u, u, pool_w, pool_scale.reshape(1, width))


def _hgrn_kernel(q_ref, f_ref, i_ref, gate_ref, lbp_ref, ng_ref, cum_ref,
                 o_ref, state_ref, *, layer, heads_per_step):
    t = q_ref.shape[1]
    d = HGRN_HEAD_DIM
    gw = heads_per_step * d
    n_chunks = t // HGRN_CHUNK
    nt_dims = (((1,), (1,)), ((), ()))
    tn_dims = (((0,), (0,)), ((), ()))

    @pl.when(pl.program_id(2) == 0)
    def _():
        state_ref[...] = jnp.zeros(state_ref.shape, F32)

    lbp = lbp_ref[...]
    ex = jnp.exp(lbp - jnp.max(lbp, axis=0, keepdims=True))
    lb = jnp.sum(ex[:layer + 1], axis=0, keepdims=True) / jnp.sum(ex, axis=0, keepdims=True)

    f = lb + (1.0 - lb) * _sigmoid(f_ref[0].astype(F32))
    logf = jnp.log(f)
    k = 1.0 - f

    cum_m = cum_ref[...]
    blk = cum_m.shape[1]
    blk_chunks = blk // HGRN_CHUNK
    blocks = [slice(r, r + blk) for r in range(0, t, blk)]
    hi, lo = _split_bf16(logf)
    hilo = jnp.concatenate([hi, lo], axis=1)
    sums = [jnp.dot(cum_m, hilo[r], preferred_element_type=F32) for r in blocks]
    sums = [x[:, :gw] + x[:, gw:] for x in sums]
    bcum = jnp.concatenate([x[:blk] for x in sums], axis=0)
    decay = jnp.exp(jnp.concatenate([x[blk:blk + blk_chunks] for x in sums], axis=0))

    q_t = (q_ref[0].astype(F32) * jnp.exp(bcum)).astype(BF16)
    k_t = k * jnp.exp(-bcum)
    k_t_b = k_t.astype(BF16)
    inp_b = i_ref[0].astype(BF16)
    in_chunk = cum_m[:blk] > 0
    heads = [slice(h * d, (h + 1) * d) for h in range(heads_per_step)]
    chunks = [slice(n * HGRN_CHUNK, (n + 1) * HGRN_CHUNK) for n in range(n_chunks)]

    scores = [[lax.dot_general(q_t[r, c], k_t_b[r, c], nt_dims, preferred_element_type=F32)
               for r in blocks] for c in heads]
    scores = [[jnp.where(in_chunk, s, 0.0).astype(BF16) for s in per_head]
              for per_head in scores]
    o_intra = [jnp.concatenate([jnp.dot(s, inp_b[r, c], preferred_element_type=F32)
                                for s, r in zip(per_head, blocks)], axis=0)
               for per_head, c in zip(scores, heads)]

    k_end = [(k_t[r] * decay[n:n + 1, :]).astype(BF16) for n, r in enumerate(chunks)]
    incr = [[lax.dot_general(inp_b[r, c], k_end[n][:, c], tn_dims, preferred_element_type=F32)
             for c in heads] for n, r in enumerate(chunks)]
    states = []
    cur = [state_ref[h] for h in range(heads_per_step)]
    for n in range(n_chunks):
        states.append([s.astype(BF16) for s in cur])
        cur = [s * decay[n:n + 1, c] + u for s, c, u in zip(cur, heads, incr[n])]
    for h in range(heads_per_step):
        state_ref[h] = cur[h]

    o_inter = [jnp.concatenate(
        [lax.dot_general(q_t[r, c], states[n][h], nt_dims, preferred_element_type=F32)
         for n, r in enumerate(chunks)], axis=0) for h, c in enumerate(heads)]

    for h, c in enumerate(heads):
        o = _rms_scale(o_intra[h] + o_inter[h]) * ng_ref[:, c]
        o_ref[0, :, c] = (o * _silu(gate_ref[0, :, c].astype(F32))).astype(o_ref.dtype)


def _hgrn2(u, lower_bounds, norm_g, layer, t, heads_per_step):
    b, s, six_w = u.shape
    width = six_w // 6
    d = HGRN_HEAD_DIM
    gw = heads_per_step * d
    groups = width // gw
    blk = HGRN_BLOCK
    assert t % blk == 0 and blk // HGRN_CHUNK <= BF16_SUBLANES
    pos = jnp.arange(blk, dtype=jnp.int32)
    chunk_id = pos // HGRN_CHUNK
    tril = (chunk_id[:, None] == chunk_id[None, :]) & (pos[None, :] <= pos[:, None])
    select = jnp.arange(BF16_SUBLANES, dtype=jnp.int32)[:, None] == chunk_id[None, :]
    cum_m = jnp.concatenate([tril, select], axis=0).astype(BF16)
    col = lambda base: (lambda bi, gi, si: (bi, si, base * groups + gi))
    blocks = 4 * _nbytes((t, gw), u.dtype) + _nbytes((t, gw), BF16)
    resident = (2 * _nbytes(cum_m.shape, BF16) + _nbytes((heads_per_step, d, d), F32)
                + heads_per_step * (16 * _nbytes((t, d), F32) + 2 * _nbytes((t, blk), F32)))
    return pl.pallas_call(
        functools.partial(_hgrn_kernel, layer=layer, heads_per_step=heads_per_step),
        grid=(b, groups, s // t),
        in_specs=[
            pl.BlockSpec((1, t, gw), col(2)),
            pl.BlockSpec((1, t, gw), col(3)),
            pl.BlockSpec((1, t, gw), col(4)),
            pl.BlockSpec((1, t, gw), col(5)),
            pl.BlockSpec((lower_bounds.shape[0], gw), lambda bi, gi, si: (0, gi)),
            pl.BlockSpec((1, gw), lambda bi, gi, si: (0, gi)),
            pl.BlockSpec(cum_m.shape, lambda bi, gi, si: (0, 0)),
        ],
        out_specs=pl.BlockSpec((1, t, gw), lambda bi, gi, si: (bi, si, gi)),
        out_shape=jax.ShapeDtypeStruct((b, s, width), BF16),
        scratch_shapes=[pltpu.VMEM((heads_per_step, d, d), F32)],
        compiler_params=pltpu.CompilerParams(
            dimension_semantics=("arbitrary", "arbitrary", "arbitrary"),
            vmem_limit_bytes=_vmem_limit(blocks, resident)),
        name="hgrn2",
    )(u, u, u, u, lower_bounds, norm_g.reshape(1, width), cum_m)


def _sb_weights(tiles, later_m):
    nt_dims = (((1,), (1,)), ((), ()))
    z = [lax.dot_general(q, kb, nt_dims, preferred_element_type=F32) for q, kb, _, _ in tiles]
    log_beta, log_keep_b, carry = [], [], []
    for x, (_, _, c, mask) in zip(z, tiles):
        if mask is not None:
            x = jnp.where(mask, x, SB_MASKED_LOGIT)
        soft = jnp.log(1.0 + jnp.exp2(-jnp.abs(x))) * LOG2E
        lb = jnp.minimum(x, 0.0) - soft
        lk = lb - x
        log_beta.append(lb)
        log_keep_b.append(lk.astype(BF16))
        carry.append(c + jnp.sum(lk, axis=-1, keepdims=True))
    later = [jnp.dot(lk, later_m, preferred_element_type=F32) for lk in log_keep_b]
    a = [jnp.exp2(lb + lt + t[2]).astype(BF16) for lb, lt, t in zip(log_beta, later, tiles)]
    return list(zip(carry, a))


def _sb_kernel(q_ref, k_ref, v_ref, gate_ref, m_ref, o_ref, *, n_sub, tr):
    qi = pl.program_id(2)
    later_wide = m_ref[...]
    later_m = later_wide[:tr, :tr]
    col = lax.broadcasted_iota(jnp.int32, (tr, 2 * tr), 1)
    row = lax.broadcasted_iota(jnp.int32, (tr, 2 * tr), 0)
    causal_window = col - tr < row
    causal_start = col < row

    def rows(j, n):
        return pl.ds(pl.multiple_of(j * tr, tr), n * tr)

    def write(s, acc):
        out = slice(s * tr, (s + 1) * tr)
        o_ref[0, out, :] = (acc * _silu(gate_ref[0, out, :].astype(F32))).astype(o_ref.dtype)

    def step(sequence_start):
        qs = [q_ref[0, s * tr:(s + 1) * tr, :] for s in range(n_sub)]
        blk = [qi * n_sub + s for s in range(n_sub)]
        no_carry = jnp.zeros((tr, 1), F32)
        start = [0 if (sequence_start and s == 0) else blk[s] - 1 for s in range(n_sub)]
        first = _sb_weights(
            [(qs[s], k_ref[0, rows(start[s], 2), :], no_carry,
              causal_start if (sequence_start and s == 0) else causal_window)
             for s in range(n_sub)], later_wide)
        with_second = [s for s in range(n_sub) if not (sequence_start and s < 2)]
        second = dict(zip(with_second, _sb_weights(
            [(qs[s], k_ref[0, rows(blk[s] - 2, 1), :], first[s][0], None) for s in with_second],
            later_m)))
        state = []
        for s in range(n_sub):
            if s in second:
                carry, a_far = second[s]
                weights = jnp.concatenate([a_far, first[s][1]], axis=1)
                acc = jnp.dot(weights, v_ref[0, rows(blk[s] - 2, 3), :],
                              preferred_element_type=F32)
                nxt = blk[s] - 3
            else:
                carry, weights = first[s]
                acc = jnp.dot(weights, v_ref[0, rows(start[s], 2), :], preferred_element_type=F32)
                nxt = -1
            write(s, acc)
            state.append((qs[s], nxt, carry, acc))

        worst = functools.reduce(jnp.maximum, [c for _, _, c, _ in state])

        @pl.when(jnp.max(worst) > SB_DEAD_LOG2_WEIGHT)
        def _():
            for s, (q, nxt, carry, acc) in enumerate(state):
                def cond(st):
                    j, alive, _, _ = st
                    return jnp.logical_and(j >= 0, alive)

                def body(st, q=q):
                    j, _, carry, acc = st
                    ((carry, a),) = _sb_weights([(q, k_ref[0, rows(j, 1), :], carry, None)], later_m)
                    acc = acc + jnp.dot(a, v_ref[0, rows(j, 1), :], preferred_element_type=F32)
                    return j - 1, jnp.max(carry) > SB_DEAD_LOG2_WEIGHT, carry, acc

                alive = jnp.max(carry) > SB_DEAD_LOG2_WEIGHT
                _, _, _, acc = lax.while_loop(cond, body, (nxt, alive, carry, acc))
                write(s, acc)

    @pl.when(qi == 0)
    def _():
        step(True)

    @pl.when(qi > 0)
    def _():
        step(False)


def _stick_breaking(u, b, tq, tr):
    four_h, rows, d = u.shape
    heads = four_h // 4
    dm = heads * d
    s = rows // b
    nq = s // tq
    assert s >= 2 * tr and tq % tr == 0
    row = jnp.arange(2 * tr)
    later_m = (row[:, None] > row[None, :]).astype(BF16)
    blocks = 2 * _nbytes((s, d), BF16) + 3 * _nbytes((tq, d), BF16)
    resident = _nbytes(later_m.shape, BF16) + (tq // tr) * 12 * _nbytes((tr, 2 * tr), F32)
    return pl.pallas_call(
        functools.partial(_sb_kernel, n_sub=tq // tr, tr=tr),
        grid=(b, heads, nq),
        in_specs=[
            pl.BlockSpec((1, tq, d), lambda bi, hi, qi: (hi, bi * nq + qi, 0)),
            pl.BlockSpec((1, s, d), lambda bi, hi, qi: (heads + hi, bi, 0)),
            pl.BlockSpec((1, s, d), lambda bi, hi, qi: (2 * heads + hi, bi, 0)),
            pl.BlockSpec((1, tq, d), lambda bi, hi, qi: (3 * heads + hi, bi * nq + qi, 0)),
            pl.BlockSpec(later_m.shape, lambda bi, hi, qi: (0, 0)),
        ],
        out_specs=pl.BlockSpec((1, tq, d), lambda bi, hi, qi: (bi, qi, hi)),
        out_shape=jax.ShapeDtypeStruct((b, s, dm), BF16),
        compiler_params=pltpu.CompilerParams(
            dimension_semantics=("arbitrary", "arbitrary", "arbitrary"),
            vmem_limit_bytes=_vmem_limit(blocks, resident)),
        name="stick_breaking",
    )(u, u, u, u, later_m)


def _xattn_kernel(q_ref, k_ref, v_ref, o_ref, *, heads):
    dm = q_ref.shape[2]
    hd = dm // heads
    scale = 1.0 / math.sqrt(hd)
    cols = [slice(h * hd, (h + 1) * hd) for h in range(heads)]
    s = [lax.dot_general(q_ref[0, :, c], k_ref[0, :, c], (((1,), (1,)), ((), ())),
                         preferred_element_type=F32) * scale for c in cols]
    e = [jnp.exp(x - jnp.max(x, axis=-1, keepdims=True)) for x in s]
    p = [(x * (1.0 / jnp.sum(x, axis=-1, keepdims=True))).astype(BF16) for x in e]
    o = [jnp.dot(x, v_ref[0, :, c], preferred_element_type=F32) for x, c in zip(p, cols)]
    for x, c in zip(o, cols):
        o_ref[0, :, c] = x.astype(o_ref.dtype)


def _cross_attention_core(q, kv, t):
    b, s, dm = q.shape
    mlen = kv.shape[1]
    blocks = 2 * _nbytes((t, dm), BF16) + 2 * _nbytes((mlen, dm), BF16)
    resident = 6 * _nbytes((t, mlen), F32) + 2 * _nbytes((t, dm // XA_HEADS), F32)
    return pl.pallas_call(
        functools.partial(_xattn_kernel, heads=XA_HEADS),
        grid=(b, s // t),
        in_specs=[
            pl.BlockSpec((1, t, dm), lambda bi, si: (bi, si, 0)),
            pl.BlockSpec((1, mlen, dm), lambda bi, si: (bi, 0, 0)),
            pl.BlockSpec((1, mlen, dm), lambda bi, si: (bi, 0, 1)),
        ],
        out_specs=pl.BlockSpec((1, t, dm), lambda bi, si: (bi, si, 0)),
        out_shape=jax.ShapeDtypeStruct((b, s, dm), BF16),
        compiler_params=pltpu.CompilerParams(
            dimension_semantics=("arbitrary", "arbitrary"),
            vmem_limit_bytes=_vmem_limit(blocks, resident)),
        name="cross_attention",
    )(q, kv, kv)


MM_TM = 1024
MM_TN = 1024
MM_TN_BF16_OUT = 2048
RES_TM = 512
POOL_ROWS = 1024
HGRN_ROWS = 512
HGRN_BLOCK = 256
HGRN_HEADS_PER_STEP = 4
SB_Q_ROWS = 4096
SB_K_ROWS = 128
XA_ROWS = 1024


def kernel(x, mem, norm_mix_g, norm_xa_g, norm_mem_g, final_norm_g, w_in_even, pool_w,
           pool_scale, hgrn_lower_bounds, hgrn_norm_g, w_out_even, w_in_odd, w_out_odd,
           xa_wq, xa_wkv, xa_wo):
    b, s, dm = x.shape
    mlen = mem.shape[1]
    depth = norm_mix_g.shape[0]
    rows = b * s
    xs = x.reshape(rows, dm)
    mem2 = mem.reshape(b * mlen, dm)
    sb_col_scale = jnp.concatenate([jnp.full((dm,), LOG2E / math.sqrt(SB_HEAD_DIM), F32),
                                    jnp.ones((3 * dm,), F32)])
    w_in_even_b, w_out_even_b = w_in_even.astype(BF16), w_out_even.astype(BF16)
    w_in_odd_b, w_out_odd_b = (w_in_odd * sb_col_scale).astype(BF16), w_out_odd.astype(BF16)
    wq_b, wo_b = xa_wq.astype(BF16), xa_wo.astype(BF16)
    h = None
    for l in range(depth):
        if l % 2 == 0:
            e = l // 2
            if h is None:
                u = _norm_matmul(xs, norm_mix_g[l], w_in_even_b, e, BF16, MM_TM, MM_TN_BF16_OUT)
            else:
                u = _matmul(h, w_in_even_b, e, BF16, MM_TM, MM_TN_BF16_OUT)
            u = u.reshape(b, s, -1)
            y_a = _pool_mixer(u, pool_w[e].astype(BF16), pool_scale[e], POOL_ROWS)
            y_b = _hgrn2(u, hgrn_lower_bounds, hgrn_norm_g[e], e, HGRN_ROWS,
                         HGRN_HEADS_PER_STEP)
            mixed = [y_a.reshape(rows, -1), y_b.reshape(rows, -1)]
            w_out, w_out_layer = w_out_even_b, e
        else:
            o = l // 2
            assert h is not None, "stick-breaking layers follow a layer that emits h"
            u = _matmul(h, w_in_odd_b, o, BF16, MM_TM, MM_TN_BF16_OUT, head_dim=SB_HEAD_DIM)
            mixed = [_stick_breaking(u, b, SB_Q_ROWS, SB_K_ROWS).reshape(rows, dm)]
            w_out, w_out_layer = w_out_odd_b, o
        xs, h_xa = _matmul_residual(mixed, w_out, w_out_layer, xs, norm_xa_g[l], RES_TM, "emit")
        q = _matmul(h_xa, wq_b, l, BF16, MM_TM, MM_TN_BF16_OUT)
        kv = _norm_matmul(mem2, norm_mem_g[l], xa_wkv, l, BF16, b * mlen, MM_TN)
        att = _cross_attention_core(q.reshape(b, s, dm), kv.reshape(b, mlen, 2 * dm), XA_ROWS)
        att = [att.reshape(rows, dm)]
        if l == depth - 1:
            (xs,) = _matmul_residual(att, wo_b, l, xs, final_norm_g, RES_TM, "final")
        else:
            xs, h = _matmul_residual(att, wo_b, l, xs, norm_mix_g[l + 1], RES_TM, "emit")
    return xs.reshape(b, s, dm)
```

```python
import functools
import math

import jax
import jax.numpy as jnp
from jax import lax
from jax.experimental import pallas as pl
from jax.experimental.pallas import tpu as pltpu

F32 = jnp.float32
BF16 = jnp.bfloat16

EPS = 1e-6
LOG2E = math.log2(math.e)
POOL_WINDOWS = (2, 4, 8, 16)
POOL_HALO = 32
F32_SUBLANES = 8
HGRN_HEAD_DIM = 128
HGRN_CHUNK = 32
BF16_SUBLANES = 16
SB_HEAD_DIM = 128
XA_HEADS = 4

V7X_VMEM_BYTES = 64 * 1024 * 1024
VMEM_CAP_BYTES = V7X_VMEM_BYTES - 8 * 1024 * 1024
COMPILER_SCRATCH_BYTES = 8 * 1024 * 1024

SB_DEAD_LOG2_WEIGHT = -151.0
SB_MASKED_LOGIT = -1e30


def _vmem_limit(pipelined_bytes, resident_bytes=0):
    need = 2 * pipelined_bytes + resident_bytes + COMPILER_SCRATCH_BYTES
    return int(min(VMEM_CAP_BYTES, need))


def _nbytes(shape, dtype):
    return math.prod(shape) * jnp.dtype(dtype).itemsize


def _sigmoid(x):
    return 0.5 * jnp.tanh(0.5 * x) + 0.5


def _silu(x):
    return x * _sigmoid(x)


def _split_bf16(x):
    hi = x.astype(BF16)
    lo = (x - hi.astype(F32)).astype(BF16)
    return hi, lo


def _rms_scale(x):
    return x * lax.rsqrt(jnp.mean(x * x, axis=-1, keepdims=True) + EPS)


NORM_ROWS = 256


def _norm_matmul_kernel(x_ref, g_ref, w_ref, o_ref, h_ref):
    tm = x_ref.shape[0]

    @pl.when(pl.program_id(1) == 0)
    def _():
        g = g_ref[...]
        w = w_ref[...].astype(h_ref.dtype)
        for r in range(0, tm, NORM_ROWS):
            rows = slice(r, r + NORM_ROWS)
            h = (_rms_scale(x_ref[rows, :]) * g).astype(h_ref.dtype)
            h_ref[rows, :] = h
            o_ref[rows, :] = jnp.dot(h, w, preferred_element_type=F32).astype(o_ref.dtype)

    @pl.when(pl.program_id(1) > 0)
    def _():
        o_ref[...] = jnp.dot(h_ref[...], w_ref[...].astype(h_ref.dtype),
                             preferred_element_type=F32).astype(o_ref.dtype)


def _norm_matmul(x, g, w, layer, out_dtype, tm, tn):
    m, k = x.shape
    n = w.shape[2]
    assert m % tm == 0 and n % tn == 0 and tm % NORM_ROWS == 0
    blocks = (_nbytes((tm, k), x.dtype) + _nbytes((k, tn), w.dtype)
              + _nbytes((tm, tn), out_dtype))
    scratch = _nbytes((tm, k), BF16) + _nbytes((tm, tn), F32)
    return pl.pallas_call(
        _norm_matmul_kernel,
        grid=(m // tm, n // tn),
        in_specs=[
            pl.BlockSpec((tm, k), lambda i, j: (i, 0)),
            pl.BlockSpec((1, k), lambda i, j: (0, 0)),
            pl.BlockSpec((None, k, tn), lambda i, j: (layer, 0, j)),
        ],
        out_specs=pl.BlockSpec((tm, tn), lambda i, j: (i, j)),
        out_shape=jax.ShapeDtypeStruct((m, n), out_dtype),
        scratch_shapes=[pltpu.VMEM((tm, k), BF16)],
        compiler_params=pltpu.CompilerParams(
            dimension_semantics=("arbitrary", "arbitrary"),
            vmem_limit_bytes=_vmem_limit(blocks, scratch)),
        name="norm_matmul",
    )(x, g.reshape(1, k), w)


def _matmul_kernel(a_ref, w_ref, o_ref):
    res = jnp.dot(a_ref[...], w_ref[...], preferred_element_type=F32).astype(o_ref.dtype)
    if len(o_ref.shape) == 2:
        o_ref[...] = res
    else:
        head_dim = o_ref.shape[2]
        for hh in range(o_ref.shape[0]):
            o_ref[hh] = res[:, hh * head_dim:(hh + 1) * head_dim]


def _matmul(a, w, layer, out_dtype, tm, tn, head_dim=None):
    m, k = a.shape
    n = w.shape[2]
    assert m % tm == 0 and n % tn == 0
    blocks = (_nbytes((tm, k), a.dtype) + _nbytes((k, tn), w.dtype)
              + _nbytes((tm, tn), out_dtype))
    if head_dim is None:
        out_spec = pl.BlockSpec((tm, tn), lambda i, j: (i, j))
        out_shape = jax.ShapeDtypeStruct((m, n), out_dtype)
    else:
        out_spec = pl.BlockSpec((tn // head_dim, tm, head_dim), lambda i, j: (j, i, 0))
        out_shape = jax.ShapeDtypeStruct((n // head_dim, m, head_dim), out_dtype)
    return pl.pallas_call(
        _matmul_kernel,
        grid=(m // tm, n // tn),
        in_specs=[pl.BlockSpec((tm, k), lambda i, j: (i, 0)),
                  pl.BlockSpec((None, k, tn), lambda i, j: (layer, 0, j))],
        out_specs=out_spec,
        out_shape=out_shape,
        compiler_params=pltpu.CompilerParams(
            dimension_semantics=("arbitrary", "arbitrary"),
            vmem_limit_bytes=_vmem_limit(blocks, _nbytes((tm, tn), F32))),
        name="matmul",
    )(a, w)


def _matmul_residual_kernel(*refs, n_lhs, norm_mode):
    lhs_refs = refs[:n_lhs]
    w_refs = refs[n_lhs:2 * n_lhs]
    res_ref, g_ref = refs[2 * n_lhs], refs[2 * n_lhs + 1]
    out_refs = refs[2 * n_lhs + 2:]
    acc = res_ref[...]
    for a_ref, w_ref in zip(lhs_refs, w_refs):
        acc = acc + jnp.dot(a_ref[...], w_ref[...], preferred_element_type=F32)
    normed = _rms_scale(acc) * g_ref[...]
    if norm_mode == "final":
        out_refs[0][...] = normed
    else:
        out_refs[0][...] = acc
        out_refs[1][...] = normed.astype(out_refs[1].dtype)


def _matmul_residual(lhs_list, w, layer, res, g, tm, norm_mode):
    assert norm_mode in ("emit", "final")
    m, n = res.shape
    kp = lhs_list[0].shape[1]
    assert all(a.shape == (m, kp) for a in lhs_list)
    assert w.shape[1:] == (kp * len(lhs_list), n)
    n_lhs = len(lhs_list)
    row_block = pl.BlockSpec((tm, n), lambda i: (i, 0))
    in_specs = [pl.BlockSpec((tm, kp), lambda i: (i, 0)) for _ in lhs_list]
    in_specs += [pl.BlockSpec((None, kp, n), functools.partial(lambda i, p: (layer, p, 0), p=p))
                 for p in range(n_lhs)]
    in_specs += [row_block, pl.BlockSpec((1, n), lambda i: (0, 0))]
    out_shape = [jax.ShapeDtypeStruct((m, n), F32)]
    if norm_mode == "emit":
        out_shape += [jax.ShapeDtypeStruct((m, n), BF16)]
    blocks = (n_lhs * (_nbytes((tm, kp), BF16) + _nbytes((kp, n), BF16))
              + 2 * _nbytes((tm, n), F32) + _nbytes((tm, n), BF16))
    return pl.pallas_call(
        functools.partial(_matmul_residual_kernel, n_lhs=n_lhs, norm_mode=norm_mode),
        grid=(m // tm,),
        in_specs=in_specs,
        out_specs=[row_block] * len(out_shape),
        out_shape=out_shape,
        compiler_params=pltpu.CompilerParams(
            dimension_semantics=("arbitrary",),
            vmem_limit_bytes=_vmem_limit(blocks, 2 * _nbytes((tm, n), F32))),
        name="matmul_residual",
    )(*lhs_list, *([w] * n_lhs), res, g.reshape(1, n))


def _pool_kernel(a_ref, gate_ref, w_ref, scale_ref, o_ref, buf_ref, part_a, part_b, *, group):
    t = a_ref.shape[1]
    end = POOL_HALO + t
    sblk = pl.program_id(1)

    @pl.when(sblk == 0)
    def _():
        buf_ref[0:POOL_HALO, :] = jnp.zeros((POOL_HALO, buf_ref.shape[1]), F32)

    buf_ref[POOL_HALO:end, :] = a_ref[0].astype(F32)
    pos = sblk * t + lax.broadcasted_iota(jnp.int32, (t, 1), 0)
    for gi, win in enumerate(POOL_WINDOWS):
        cols = slice(gi * group, (gi + 1) * group)
        cur = buf_ref[POOL_HALO:end, cols]
        levels = win.bit_length() - 1
        load = lambda lo, hi, cols=cols: buf_ref[lo:hi, cols]
        for k in range(1, levels + 1):
            shift = 2 ** (k - 1)
            lo = POOL_HALO - F32_SUBLANES * (levels - k)
            tot = load(lo, end) + load(lo - shift, end - shift)
            if k < levels:
                part = part_a if k % 2 else part_b
                part[lo:end, :] = tot
                load = lambda lo, hi, part=part: part[lo:hi, :]
        cnt = jnp.minimum(pos + 1, win).astype(F32)
        mixed = tot / cnt - cur
        y = jnp.dot(mixed.astype(BF16), w_ref[gi], preferred_element_type=F32)
        y = y * scale_ref[:, cols]
        o_ref[0, :, cols] = (y * _silu(gate_ref[0, :, cols].astype(F32))).astype(o_ref.dtype)
    buf_ref[0:POOL_HALO, :] = buf_ref[t:t + POOL_HALO, :]


def _pool_mixer(u, pool_w, pool_scale, t):
    b, s, _ = u.shape
    n_groups, group, _ = pool_w.shape
    width = n_groups * group
    blocks = 2 * _nbytes((t, width), u.dtype) + _nbytes((t, width), BF16)
    assert POOL_HALO >= F32_SUBLANES * (max(POOL_WINDOWS).bit_length() - 1)
    resident = (_nbytes((t + POOL_HALO, width), F32) + _nbytes(pool_w.shape, BF16)
                + 6 * _nbytes((t + POOL_HALO, group), F32))
    return pl.pallas_call(
        functools.partial(_pool_kernel, group=group),
        grid=(b, s // t),
        in_specs=[
            pl.BlockSpec((1, t, width), lambda bi, si: (bi, si, 0)),
            pl.BlockSpec((1, t, width), lambda bi, si: (bi, si, 1)),
            pl.BlockSpec(pool_w.shape, lambda bi, si: (0, 0, 0)),
            pl.BlockSpec((1, width), lambda bi, si: (0, 0)),
        ],
        out_specs=pl.BlockSpec((1, t, width), lambda bi, si: (bi, si, 0)),
        out_shape=jax.ShapeDtypeStruct((b, s, width), BF16),
        scratch_shapes=[pltpu.VMEM((t + POOL_HALO, width), F32),
                        pltpu.VMEM((t + POOL_HALO, group), F32),
                        pltpu.VMEM((t + POOL_HALO, group), F32)],
        compiler_params=pltpu.CompilerParams(
            dimension_semantics=("arbitrary", "arbitrary"),
            vmem_limit_bytes=_vmem_limit(blocks, resident)),
        name="pool_mixer",
    )(u, u, pool_w, pool_scale.reshape(1, width))


def _hgrn_kernel(q_ref, f_ref, i_ref, gate_ref, lbp_ref, ng_ref, cum_ref,
                 o_ref, state_ref, *, layer, heads_per_step):
    t = q_ref.shape[1]
    d = HGRN_HEAD_DIM
    gw = heads_per_step * d
    n_chunks = t // HGRN_CHUNK
    nt_dims = (((1,), (1,)), ((), ()))
    tn_dims = (((0,), (0,)), ((), ()))

    @pl.when(pl.program_id(2) == 0)
    def _():
        state_ref[...] = jnp.zeros(state_ref.shape, F32)

    lbp = lbp_ref[...]
    ex = jnp.exp(lbp - jnp.max(lbp, axis=0, keepdims=True))
    lb = jnp.sum(ex[:layer + 1], axis=0, keepdims=True) / jnp.sum(ex, axis=0, keepdims=True)

    f = lb + (1.0 - lb) * _sigmoid(f_ref[0].astype(F32))
    logf = jnp.log(f)
    k = 1.0 - f

    cum_m = cum_ref[...]
    blk = cum_m.shape[1]
    blk_chunks = blk // HGRN_CHUNK
    blocks = [slice(r, r + blk) for r in range(0, t, blk)]
    hi, lo = _split_bf16(logf)
    hilo = jnp.concatenate([hi, lo], axis=1)
    sums = [jnp.dot(cum_m, hilo[r], preferred_element_type=F32) for r in blocks]
    sums = [x[:, :gw] + x[:, gw:] for x in sums]
    bcum = jnp.concatenate([x[:blk] for x in sums], axis=0)
    decay = jnp.exp(jnp.concatenate([x[blk:blk + blk_chunks] for x in sums], axis=0))

    q_t = (q_ref[0].astype(F32) * jnp.exp(bcum)).astype(BF16)
    k_t = k * jnp.exp(-bcum)
    k_t_b = k_t.astype(BF16)
    inp_b = i_ref[0].astype(BF16)
    in_chunk = cum_m[:blk] > 0
    heads = [slice(h * d, (h + 1) * d) for h in range(heads_per_step)]
    chunks = [slice(n * HGRN_CHUNK, (n + 1) * HGRN_CHUNK) for n in range(n_chunks)]

    scores = [[lax.dot_general(q_t[r, c], k_t_b[r, c], nt_dims, preferred_element_type=F32)
               for r in blocks] for c in heads]
    scores = [[jnp.where(in_chunk, s, 0.0).astype(BF16) for s in per_head]
              for per_head in scores]
    o_intra = [jnp.concatenate([jnp.dot(s, inp_b[r, c], preferred_element_type=F32)
                                for s, r in zip(per_head, blocks)], axis=0)
               for per_head, c in zip(scores, heads)]

    k_end = [(k_t[r] * decay[n:n + 1, :]).astype(BF16) for n, r in enumerate(chunks)]
    incr = [[lax.dot_general(inp_b[r, c], k_end[n][:, c], tn_dims, preferred_element_type=F32)
             for c in heads] for n, r in enumerate(chunks)]
    states = []
    cur = [state_ref[h] for h in range(heads_per_step)]
    for n in range(n_chunks):
        states.append([s.astype(BF16) for s in cur])
        cur = [s * decay[n:n + 1, c] + u for s, c, u in zip(cur, heads, incr[n])]
    for h in range(heads_per_step):
        state_ref[h] = cur[h]

    o_inter = [jnp.concatenate(
        [lax.dot_general(q_t[r, c], states[n][h], nt_dims, preferred_element_type=F32)
         for n, r in enumerate(chunks)], axis=0) for h, c in enumerate(heads)]

    for h, c in enumerate(heads):
        o = _rms_scale(o_intra[h] + o_inter[h]) * ng_ref[:, c]
        o_ref[0, :, c] = (o * _silu(gate_ref[0, :, c].astype(F32))).astype(o_ref.dtype)


def _hgrn2(u, lower_bounds, norm_g, layer, t, heads_per_step):
    b, s, six_w = u.shape
    width = six_w // 6
    d = HGRN_HEAD_DIM
    gw = heads_per_step * d
    groups = width // gw
    blk = HGRN_BLOCK
    assert t % blk == 0 and blk // HGRN_CHUNK <= BF16_SUBLANES
    pos = jnp.arange(blk, dtype=jnp.int32)
    chunk_id = pos // HGRN_CHUNK
    tril = (chunk_id[:, None] == chunk_id[None, :]) & (pos[None, :] <= pos[:, None])
    select = jnp.arange(BF16_SUBLANES, dtype=jnp.int32)[:, None] == chunk_id[None, :]
    cum_m = jnp.concatenate([tril, select], axis=0).astype(BF16)
    col = lambda base: (lambda bi, gi, si: (bi, si, base * groups + gi))
    blocks = 4 * _nbytes((t, gw), u.dtype) + _nbytes((t, gw), BF16)
    resident = (2 * _nbytes(cum_m.shape, BF16) + _nbytes((heads_per_step, d, d), F32)
                + heads_per_step * (16 * _nbytes((t, d), F32) + 2 * _nbytes((t, blk), F32)))
    return pl.pallas_call(
        functools.partial(_hgrn_kernel, layer=layer, heads_per_step=heads_per_step),
        grid=(b, groups, s // t),
        in_specs=[
            pl.BlockSpec((1, t, gw), col(2)),
            pl.BlockSpec((1, t, gw), col(3)),
            pl.BlockSpec((1, t, gw), col(4)),
            pl.BlockSpec((1, t, gw), col(5)),
            pl.BlockSpec((lower_bounds.shape[0], gw), lambda bi, gi, si: (0, gi)),
            pl.BlockSpec((1, gw), lambda bi, gi, si: (0, gi)),
            pl.BlockSpec(cum_m.shape, lambda bi, gi, si: (0, 0)),
        ],
        out_specs=pl.BlockSpec((1, t, gw), lambda bi, gi, si: (bi, si, gi)),
        out_shape=jax.ShapeDtypeStruct((b, s, width), BF16),
        scratch_shapes=[pltpu.VMEM((heads_per_step, d, d), F32)],
        compiler_params=pltpu.CompilerParams(
            dimension_semantics=("arbitrary", "arbitrary", "arbitrary"),
            vmem_limit_bytes=_vmem_limit(blocks, resident)),
        name="hgrn2",
    )(u, u, u, u, lower_bounds, norm_g.reshape(1, width), cum_m)


def _sb_weights(tiles, later_m):
    nt_dims = (((1,), (1,)), ((), ()))
    z = [lax.dot_general(q, kb, nt_dims, preferred_element_type=F32) for q, kb, _, _ in tiles]
    log_beta, log_keep_b, carry = [], [], []
    for x, (_, _, c, mask) in zip(z, tiles):
        if mask is not None:
            x = jnp.where(mask, x, SB_MASKED_LOGIT)
        soft = jnp.log(1.0 + jnp.exp2(-jnp.abs(x))) * LOG2E
        lb = jnp.minimum(x, 0.0) - soft
        lk = lb - x
        log_beta.append(lb)
        log_keep_b.append(lk.astype(BF16))
        carry.append(c + jnp.sum(lk, axis=-1, keepdims=True))
    later = [jnp.dot(lk, later_m, preferred_element_type=F32) for lk in log_keep_b]
    a = [jnp.exp2(lb + lt + t[2]).astype(BF16) for lb, lt, t in zip(log_beta, later, tiles)]
    return list(zip(carry, a))


def _sb_kernel(q_ref, k_ref, v_ref, gate_ref, m_ref, o_ref, *, n_sub, tr):
    qi = pl.program_id(2)
    later_wide = m_ref[...]
    later_m = later_wide[:tr, :tr]
    col = lax.broadcasted_iota(jnp.int32, (tr, 2 * tr), 1)
    row = lax.broadcasted_iota(jnp.int32, (tr, 2 * tr), 0)
    causal_window = col - tr < row
    causal_start = col < row

    def rows(j, n):
        return pl.ds(pl.multiple_of(j * tr, tr), n * tr)

    def write(out, acc):
        o_ref[0, out, :] = (acc * _silu(gate_ref[0, out, :].astype(F32))).astype(o_ref.dtype)

    def step(sequence_start):
        half = tr // 2
        qs = [q_ref[0, s * tr:(s + 1) * tr, :] for s in range(n_sub)]
        blk = [qi * n_sub + s for s in range(n_sub)]
        no_carry = jnp.zeros((tr, 1), F32)
        start = [0 if (sequence_start and s == 0) else blk[s] - 1 for s in range(n_sub)]
        first = _sb_weights(
            [(qs[s], k_ref[0, rows(start[s], 2), :], no_carry,
              causal_start if (sequence_start and s == 0) else causal_window)
             for s in range(n_sub)], later_wide)
        with_second = [s for s in range(n_sub) if not (sequence_start and s < 2)]
        second = dict(zip(with_second, _sb_weights(
            [(qs[s][:half], k_ref[0, rows(blk[s] - 2, 1), :], first[s][0][:half], None)
             for s in with_second], later_m)))
        state = []
        for s in range(n_sub):
            carry, weights = first[s]
            if s in second:
                carry_top, a_far = second[s]
                top = jnp.dot(jnp.concatenate([a_far, weights[:half]], axis=1),
                              v_ref[0, rows(blk[s] - 2, 3), :], preferred_element_type=F32)
                bottom = jnp.dot(weights[half:], v_ref[0, rows(start[s], 2), :],
                                 preferred_element_type=F32)
                state.append((qs[s][:half], slice(s * tr, s * tr + half), blk[s] - 3,
                              carry_top, top))
                state.append((qs[s][half:], slice(s * tr + half, (s + 1) * tr), blk[s] - 2,
                              carry[half:], bottom))
            else:
                acc = jnp.dot(weights, v_ref[0, rows(start[s], 2), :], preferred_element_type=F32)
                state.append((qs[s], slice(s * tr, (s + 1) * tr), -1, carry, acc))
        for _, out, _, _, acc in state:
            write(out, acc)

        worst = jnp.max(jnp.concatenate([c for _, _, _, c, _ in state], axis=0))

        @pl.when(worst > SB_DEAD_LOG2_WEIGHT)
        def _():
            for q, out, nxt, carry, acc in state:
                def cond(st):
                    j, alive, _, _ = st
                    return jnp.logical_and(j >= 0, alive)

                def body(st, q=q):
                    j, _, carry, acc = st
                    ((carry, a),) = _sb_weights([(q, k_ref[0, rows(j, 1), :], carry, None)], later_m)
                    acc = acc + jnp.dot(a, v_ref[0, rows(j, 1), :], preferred_element_type=F32)
                    return j - 1, jnp.max(carry) > SB_DEAD_LOG2_WEIGHT, carry, acc

                alive = jnp.max(carry) > SB_DEAD_LOG2_WEIGHT
                _, _, _, acc = lax.while_loop(cond, body, (nxt, alive, carry, acc))
                write(out, acc)

    @pl.when(qi == 0)
    def _():
        step(True)

    @pl.when(qi > 0)
    def _():
        step(False)


def _stick_breaking(u, b, tq, tr):
    four_h, rows, d = u.shape
    heads = four_h // 4
    dm = heads * d
    s = rows // b
    nq = s // tq
    assert s >= 2 * tr and tq % tr == 0
    row = jnp.arange(2 * tr)
    later_m = (row[:, None] > row[None, :]).astype(BF16)
    blocks = 2 * _nbytes((s, d), BF16) + 3 * _nbytes((tq, d), BF16)
    resident = _nbytes(later_m.shape, BF16) + (tq // tr) * 12 * _nbytes((tr, 2 * tr), F32)
    return pl.pallas_call(
        functools.partial(_sb_kernel, n_sub=tq // tr, tr=tr),
        grid=(b, heads, nq),
        in_specs=[
            pl.BlockSpec((1, tq, d), lambda bi, hi, qi: (hi, bi * nq + qi, 0)),
            pl.BlockSpec((1, s, d), lambda bi, hi, qi: (heads + hi, bi, 0)),
            pl.BlockSpec((1, s, d), lambda bi, hi, qi: (2 * heads + hi, bi, 0)),
            pl.BlockSpec((1, tq, d), lambda bi, hi, qi: (3 * heads + hi, bi * nq + qi, 0)),
            pl.BlockSpec(later_m.shape, lambda bi, hi, qi: (0, 0)),
        ],
        out_specs=pl.BlockSpec((1, tq, d), lambda bi, hi, qi: (bi, qi, hi)),
        out_shape=jax.ShapeDtypeStruct((b, s, dm), BF16),
        compiler_params=pltpu.CompilerParams(
            dimension_semantics=("arbitrary", "arbitrary", "arbitrary"),
            vmem_limit_bytes=_vmem_limit(blocks, resident)),
        name="stick_breaking",
    )(u, u, u, u, later_m)


def _xattn_kernel(q_ref, k_ref, v_ref, o_ref, *, heads):
    dm = q_ref.shape[2]
    hd = dm // heads
    scale = 1.0 / math.sqrt(hd)
    cols = [slice(h * hd, (h + 1) * hd) for h in range(heads)]
    s = [lax.dot_general(q_ref[0, :, c], k_ref[0, :, c], (((1,), (1,)), ((), ())),
                         preferred_element_type=F32) * scale for c in cols]
    e = [jnp.exp(x - jnp.max(x, axis=-1, keepdims=True)) for x in s]
    p = [(x * (1.0 / jnp.sum(x, axis=-1, keepdims=True))).astype(BF16) for x in e]
    o = [jnp.dot(x, v_ref[0, :, c], preferred_element_type=F32) for x, c in zip(p, cols)]
    for x, c in zip(o, cols):
        o_ref[0, :, c] = x.astype(o_ref.dtype)


def _cross_attention_core(q, kv, t):
    b, s, dm = q.shape
    mlen = kv.shape[1]
    blocks = 2 * _nbytes((t, dm), BF16) + 2 * _nbytes((mlen, dm), BF16)
    resident = 6 * _nbytes((t, mlen), F32) + 2 * _nbytes((t, dm // XA_HEADS), F32)
    return pl.pallas_call(
        functools.partial(_xattn_kernel, heads=XA_HEADS),
        grid=(b, s // t),
        in_specs=[
            pl.BlockSpec((1, t, dm), lambda bi, si: (bi, si, 0)),
            pl.BlockSpec((1, mlen, dm), lambda bi, si: (bi, 0, 0)),
            pl.BlockSpec((1, mlen, dm), lambda bi, si: (bi, 0, 1)),
        ],
        out_specs=pl.BlockSpec((1, t, dm), lambda bi, si: (bi, si, 0)),
        out_shape=jax.ShapeDtypeStruct((b, s, dm), BF16),
        compiler_params=pltpu.CompilerParams(
            dimension_semantics=("arbitrary", "arbitrary"),
            vmem_limit_bytes=_vmem_limit(blocks, resident)),
        name="cross_attention",
    )(q, kv, kv)


MM_TM = 1024
MM_TN = 1024
MM_TN_BF16_OUT = 2048
RES_TM = 512
POOL_ROWS = 1024
HGRN_ROWS = 512
HGRN_BLOCK = 256
HGRN_HEADS_PER_STEP = 4
SB_Q_ROWS = 2048
SB_K_ROWS = 128
XA_ROWS = 1024


def kernel(x, mem, norm_mix_g, norm_xa_g, norm_mem_g, final_norm_g, w_in_even, pool_w,
           pool_scale, hgrn_lower_bounds, hgrn_norm_g, w_out_even, w_in_odd, w_out_odd,
           xa_wq, xa_wkv, xa_wo):
    b, s, dm = x.shape
    mlen = mem.shape[1]
    depth = norm_mix_g.shape[0]
    rows = b * s
    xs = x.reshape(rows, dm)
    mem2 = mem.reshape(b * mlen, dm)
    sb_col_scale = jnp.concatenate([jnp.full((dm,), LOG2E / math.sqrt(SB_HEAD_DIM), F32),
                                    jnp.ones((3 * dm,), F32)])
    w_in_even_b, w_out_even_b = w_in_even.astype(BF16), w_out_even.astype(BF16)
    w_in_odd_b, w_out_odd_b = (w_in_odd * sb_col_scale).astype(BF16), w_out_odd.astype(BF16)
    wq_b, wo_b = xa_wq.astype(BF16), xa_wo.astype(BF16)
    h = None
    for l in range(depth):
        if l % 2 == 0:
            e = l // 2
            if h is None:
                u = _norm_matmul(xs, norm_mix_g[l], w_in_even_b, e, BF16, MM_TM, MM_TN_BF16_OUT)
            else:
                u = _matmul(h, w_in_even_b, e, BF16, MM_TM, MM_TN_BF16_OUT)
            u = u.reshape(b, s, -1)
            y_a = _pool_mixer(u, pool_w[e].astype(BF16), pool_scale[e], POOL_ROWS)
            y_b = _hgrn2(u, hgrn_lower_bounds, hgrn_norm_g[e], e, HGRN_ROWS,
                         HGRN_HEADS_PER_STEP)
            mixed = [y_a.reshape(rows, -1), y_b.reshape(rows, -1)]
            w_out, w_out_layer = w_out_even_b, e
        else:
            o = l // 2
            assert h is not None, "stick-breaking layers follow a layer that emits h"
            u = _matmul(h, w_in_odd_b, o, BF16, MM_TM, MM_TN_BF16_OUT, head_dim=SB_HEAD_DIM)
            mixed = [_stick_breaking(u, b, SB_Q_ROWS, SB_K_ROWS).reshape(rows, dm)]
            w_out, w_out_layer = w_out_odd_b, o
        xs, h_xa = _matmul_residual(mixed, w_out, w_out_layer, xs, norm_xa_g[l], RES_TM, "emit")
        q = _matmul(h_xa, wq_b, l, BF16, MM_TM, MM_TN_BF16_OUT)
        kv = _norm_matmul(mem2, norm_mem_g[l], xa_wkv, l, BF16, b * mlen, MM_TN)
        att = _cross_attention_core(q.reshape(b, s, dm), kv.reshape(b, mlen, 2 * dm), XA_ROWS)
        att = [att.reshape(rows, dm)]
        if l == depth - 1:
            (xs,) = _matmul_residual(att, wo_b, l, xs, final_norm_g, RES_TM, "final")
        else:
            xs, h = _matmul_residual(att, wo_b, l, xs, norm_mix_g[l + 1], RES_TM, "emit")
    return xs.reshape(b, s, dm)
```

```python
import functools
import math

import jax
import jax.numpy as jnp
from jax import lax
from jax.experimental import pallas as pl
from jax.experimental.pallas import tpu as pltpu

F32 = jnp.float32
BF16 = jnp.bfloat16

EPS = 1e-6
LOG2E = math.log2(math.e)
POOL_WINDOWS = (2, 4, 8, 16)
POOL_HALO = 32
F32_SUBLANES = 8
HGRN_HEAD_DIM = 128
HGRN_CHUNK = 32
BF16_SUBLANES = 16
SB_HEAD_DIM = 128
XA_HEADS = 4

V7X_VMEM_BYTES = 64 * 1024 * 1024
VMEM_CAP_BYTES = V7X_VMEM_BYTES - 8 * 1024 * 1024
COMPILER_SCRATCH_BYTES = 8 * 1024 * 1024

SB_DEAD_LOG2_WEIGHT = -151.0
SB_MASKED_LOGIT = -1e30


def _vmem_limit(pipelined_bytes, resident_bytes=0):
    need = 2 * pipelined_bytes + resident_bytes + COMPILER_SCRATCH_BYTES
    return int(min(VMEM_CAP_BYTES, need))


def _nbytes(shape, dtype):
    return math.prod(shape) * jnp.dtype(dtype).itemsize


def _sigmoid(x):
    return 0.5 * jnp.tanh(0.5 * x) + 0.5


def _silu(x):
    return x * _sigmoid(x)


def _split_bf16(x):
    hi = x.astype(BF16)
    lo = (x - hi.astype(F32)).astype(BF16)
    return hi, lo


def _rms_scale(x):
    return x * lax.rsqrt(jnp.mean(x * x, axis=-1, keepdims=True) + EPS)


NORM_ROWS = 256


def _norm_matmul_kernel(x_ref, g_ref, w_ref, o_ref, h_ref):
    tm = x_ref.shape[0]

    @pl.when(pl.program_id(1) == 0)
    def _():
        g = g_ref[...]
        w = w_ref[...].astype(h_ref.dtype)
        for r in range(0, tm, NORM_ROWS):
            rows = slice(r, r + NORM_ROWS)
            h = (_rms_scale(x_ref[rows, :]) * g).astype(h_ref.dtype)
            h_ref[rows, :] = h
            o_ref[rows, :] = jnp.dot(h, w, preferred_element_type=F32).astype(o_ref.dtype)

    @pl.when(pl.program_id(1) > 0)
    def _():
        o_ref[...] = jnp.dot(h_ref[...], w_ref[...].astype(h_ref.dtype),
                             preferred_element_type=F32).astype(o_ref.dtype)


def _norm_matmul(x, g, w, layer, out_dtype, tm, tn):
    m, k = x.shape
    n = w.shape[2]
    assert m % tm == 0 and n % tn == 0 and tm % NORM_ROWS == 0
    blocks = (_nbytes((tm, k), x.dtype) + _nbytes((k, tn), w.dtype)
              + _nbytes((tm, tn), out_dtype))
    scratch = _nbytes((tm, k), BF16) + _nbytes((tm, tn), F32)
    return pl.pallas_call(
        _norm_matmul_kernel,
        grid=(m // tm, n // tn),
        in_specs=[
            pl.BlockSpec((tm, k), lambda i, j: (i, 0)),
            pl.BlockSpec((1, k), lambda i, j: (0, 0)),
            pl.BlockSpec((None, k, tn), lambda i, j: (layer, 0, j)),
        ],
        out_specs=pl.BlockSpec((tm, tn), lambda i, j: (i, j)),
        out_shape=jax.ShapeDtypeStruct((m, n), out_dtype),
        scratch_shapes=[pltpu.VMEM((tm, k), BF16)],
        compiler_params=pltpu.CompilerParams(
            dimension_semantics=("arbitrary", "arbitrary"),
            vmem_limit_bytes=_vmem_limit(blocks, scratch)),
        name="norm_matmul",
    )(x, g.reshape(1, k), w)


def _matmul_kernel(a_ref, w_ref, o_ref):
    res = jnp.dot(a_ref[...], w_ref[...], preferred_element_type=F32).astype(o_ref.dtype)
    if len(o_ref.shape) == 2:
        o_ref[...] = res
    else:
        head_dim = o_ref.shape[2]
        for hh in range(o_ref.shape[0]):
            o_ref[hh] = res[:, hh * head_dim:(hh + 1) * head_dim]


def _matmul(a, w, layer, out_dtype, tm, tn, head_dim=None):
    m, k = a.shape
    n = w.shape[2]
    assert m % tm == 0 and n % tn == 0
    blocks = (_nbytes((tm, k), a.dtype) + _nbytes((k, tn), w.dtype)
              + _nbytes((tm, tn), out_dtype))
    if head_dim is None:
        out_spec = pl.BlockSpec((tm, tn), lambda i, j: (i, j))
        out_shape = jax.ShapeDtypeStruct((m, n), out_dtype)
    else:
        out_spec = pl.BlockSpec((tn // head_dim, tm, head_dim), lambda i, j: (j, i, 0))
        out_shape = jax.ShapeDtypeStruct((n // head_dim, m, head_dim), out_dtype)
    return pl.pallas_call(
        _matmul_kernel,
        grid=(m // tm, n // tn),
        in_specs=[pl.BlockSpec((tm, k), lambda i, j: (i, 0)),
                  pl.BlockSpec((None, k, tn), lambda i, j: (layer, 0, j))],
        out_specs=out_spec,
        out_shape=out_shape,
        compiler_params=pltpu.CompilerParams(
            dimension_semantics=("arbitrary", "arbitrary"),
            vmem_limit_bytes=_vmem_limit(blocks, _nbytes((tm, tn), F32))),
        name="matmul",
    )(a, w)


def _matmul_residual_kernel(*refs, n_lhs, norm_mode):
    lhs_refs = refs[:n_lhs]
    w_refs = refs[n_lhs:2 * n_lhs]
    res_ref, g_ref = refs[2 * n_lhs], refs[2 * n_lhs + 1]
    out_refs = refs[2 * n_lhs + 2:]
    acc = res_ref[...]
    for a_ref, w_ref in zip(lhs_refs, w_refs):
        acc = acc + jnp.dot(a_ref[...], w_ref[...], preferred_element_type=F32)
    normed = _rms_scale(acc) * g_ref[...]
    if norm_mode == "final":
        out_refs[0][...] = normed
    else:
        out_refs[0][...] = acc
        out_refs[1][...] = normed.astype(out_refs[1].dtype)


def _matmul_residual(lhs_list, w, layer, res, g, tm, norm_mode):
    assert norm_mode in ("emit", "final")
    m, n = res.shape
    kp = lhs_list[0].shape[1]
    assert all(a.shape == (m, kp) for a in lhs_list)
    assert w.shape[1:] == (kp * len(lhs_list), n)
    n_lhs = len(lhs_list)
    row_block = pl.BlockSpec((tm, n), lambda i: (i, 0))
    in_specs = [pl.BlockSpec((tm, kp), lambda i: (i, 0)) for _ in lhs_list]
    in_specs += [pl.BlockSpec((None, kp, n), functools.partial(lambda i, p: (layer, p, 0), p=p))
                 for p in range(n_lhs)]
    in_specs += [row_block, pl.BlockSpec((1, n), lambda i: (0, 0))]
    out_shape = [jax.ShapeDtypeStruct((m, n), F32)]
    if norm_mode == "emit":
        out_shape += [jax.ShapeDtypeStruct((m, n), BF16)]
    blocks = (n_lhs * (_nbytes((tm, kp), BF16) + _nbytes((kp, n), BF16))
              + 2 * _nbytes((tm, n), F32) + _nbytes((tm, n), BF16))
    return pl.pallas_call(
        functools.partial(_matmul_residual_kernel, n_lhs=n_lhs, norm_mode=norm_mode),
        grid=(m // tm,),
        in_specs=in_specs,
        out_specs=[row_block] * len(out_shape),
        out_shape=out_shape,
        compiler_params=pltpu.CompilerParams(
            dimension_semantics=("arbitrary",),
            vmem_limit_bytes=_vmem_limit(blocks, 2 * _nbytes((tm, n), F32))),
        name="matmul_residual",
    )(*lhs_list, *([w] * n_lhs), res, g.reshape(1, n))


def _pool_kernel(a_ref, gate_ref, w_ref, scale_ref, o_ref, buf_ref, part_a, part_b, *, group):
    t = a_ref.shape[1]
    end = POOL_HALO + t
    sblk = pl.program_id(1)

    @pl.when(sblk == 0)
    def _():
        buf_ref[0:POOL_HALO, :] = jnp.zeros((POOL_HALO, buf_ref.shape[1]), F32)

    buf_ref[POOL_HALO:end, :] = a_ref[0].astype(F32)
    pos = sblk * t + lax.broadcasted_iota(jnp.int32, (t, 1), 0)
    for gi, win in enumerate(POOL_WINDOWS):
        cols = slice(gi * group, (gi + 1) * group)
        cur = buf_ref[POOL_HALO:end, cols]
        levels = win.bit_length() - 1
        load = lambda lo, hi, cols=cols: buf_ref[lo:hi, cols]
        for k in range(1, levels + 1):
            shift = 2 ** (k - 1)
            lo = POOL_HALO - F32_SUBLANES * (levels - k)
            tot = load(lo, end) + load(lo - shift, end - shift)
            if k < levels:
                part = part_a if k % 2 else part_b
                part[lo:end, :] = tot
                load = lambda lo, hi, part=part: part[lo:hi, :]
        cnt = jnp.minimum(pos + 1, win).astype(F32)
        mixed = tot / cnt - cur
        y = jnp.dot(mixed.astype(BF16), w_ref[gi], preferred_element_type=F32)
        y = y * scale_ref[:, cols]
        o_ref[0, :, cols] = (y * _silu(gate_ref[0, :, cols].astype(F32))).astype(o_ref.dtype)
    buf_ref[0:POOL_HALO, :] = buf_ref[t:t + POOL_HALO, :]


def _pool_mixer(u, pool_w, pool_scale, t):
    b, s, _ = u.shape
    n_groups, group, _ = pool_w.shape
    width = n_groups * group
    blocks = 2 * _nbytes((t, width), u.dtype) + _nbytes((t, width), BF16)
    assert POOL_HALO >= F32_SUBLANES * (max(POOL_WINDOWS).bit_length() - 1)
    resident = (_nbytes((t + POOL_HALO, width), F32) + _nbytes(pool_w.shape, BF16)
                + 6 * _nbytes((t + POOL_HALO, group), F32))
    return pl.pallas_call(
        functools.partial(_pool_kernel, group=group),
        grid=(b, s // t),
        in_specs=[
            pl.BlockSpec((1, t, width), lambda bi, si: (bi, si, 0)),
            pl.BlockSpec((1, t, width), lambda bi, si: (bi, si, 1)),
            pl.BlockSpec(pool_w.shape, lambda bi, si: (0, 0, 0)),
            pl.BlockSpec((1, width), lambda bi, si: (0, 0)),
        ],
        out_specs=pl.BlockSpec((1, t, width), lambda bi, si: (bi, si, 0)),
        out_shape=jax.ShapeDtypeStruct((b, s, width), BF16),
        scratch_shapes=[pltpu.VMEM((t + POOL_HALO, width), F32),
                        pltpu.VMEM((t + POOL_HALO, group), F32),
                        pltpu.VMEM((t + POOL_HALO, group), F32)],
        compiler_params=pltpu.CompilerParams(
            dimension_semantics=("arbitrary", "arbitrary"),
            vmem_limit_bytes=_vmem_limit(blocks, resident)),
        name="pool_mixer",
    )(u, u, pool_w, pool_scale.reshape(1, width))


def _hgrn_kernel(q_ref, f_ref, i_ref, gate_ref, lbp_ref, ng_ref, cum_ref,
                 o_ref, state_ref, *, layer, heads_per_step):
    t = q_ref.shape[1]
    d = HGRN_HEAD_DIM
    gw = heads_per_step * d
    n_chunks = t // HGRN_CHUNK
    nt_dims = (((1,), (1,)), ((), ()))
    tn_dims = (((0,), (0,)), ((), ()))

    @pl.when(pl.program_id(2) == 0)
    def _():
        state_ref[...] = jnp.zeros(state_ref.shape, F32)

    lbp = lbp_ref[...]
    ex = jnp.exp(lbp - jnp.max(lbp, axis=0, keepdims=True))
    lb = jnp.sum(ex[:layer + 1], axis=0, keepdims=True) / jnp.sum(ex, axis=0, keepdims=True)

    f = lb + (1.0 - lb) * _sigmoid(f_ref[0].astype(F32))
    logf = jnp.log(f)
    k = 1.0 - f

    cum_m = cum_ref[...]
    blk = cum_m.shape[1]
    blk_chunks = blk // HGRN_CHUNK
    blocks = [slice(r, r + blk) for r in range(0, t, blk)]
    hi, lo = _split_bf16(logf)
    hilo = jnp.concatenate([hi, lo], axis=1)
    sums = [jnp.dot(cum_m, hilo[r], preferred_element_type=F32) for r in blocks]
    sums = [x[:, :gw] + x[:, gw:] for x in sums]
    bcum = jnp.concatenate([x[:blk] for x in sums], axis=0)
    decay = jnp.exp(jnp.concatenate([x[blk:blk + blk_chunks] for x in sums], axis=0))

    q_t = (q_ref[0].astype(F32) * jnp.exp(bcum)).astype(BF16)
    k_t = k * jnp.exp(-bcum)
    k_t_b = k_t.astype(BF16)
    inp_b = i_ref[0].astype(BF16)
    in_chunk = cum_m[:blk] > 0
    heads = [slice(h * d, (h + 1) * d) for h in range(heads_per_step)]
    chunks = [slice(n * HGRN_CHUNK, (n + 1) * HGRN_CHUNK) for n in range(n_chunks)]

    scores = [[lax.dot_general(q_t[r, c], k_t_b[r, c], nt_dims, preferred_element_type=F32)
               for r in blocks] for c in heads]
    scores = [[jnp.where(in_chunk, s, 0.0).astype(BF16) for s in per_head]
              for per_head in scores]
    o_intra = [jnp.concatenate([jnp.dot(s, inp_b[r, c], preferred_element_type=F32)
                                for s, r in zip(per_head, blocks)], axis=0)
               for per_head, c in zip(scores, heads)]

    k_end = [(k_t[r] * decay[n:n + 1, :]).astype(BF16) for n, r in enumerate(chunks)]
    incr = [[lax.dot_general(inp_b[r, c], k_end[n][:, c], tn_dims, preferred_element_type=F32)
             for c in heads] for n, r in enumerate(chunks)]
    states = []
    cur = [state_ref[h] for h in range(heads_per_step)]
    for n in range(n_chunks):
        states.append([s.astype(BF16) for s in cur])
        cur = [s * decay[n:n + 1, c] + u for s, c, u in zip(cur, heads, incr[n])]
    for h in range(heads_per_step):
        state_ref[h] = cur[h]

    o_inter = [jnp.concatenate(
        [lax.dot_general(q_t[r, c], states[n][h], nt_dims, preferred_element_type=F32)
         for n, r in enumerate(chunks)], axis=0) for h, c in enumerate(heads)]

    for h, c in enumerate(heads):
        o = _rms_scale(o_intra[h] + o_inter[h]) * ng_ref[:, c]
        o_ref[0, :, c] = (o * _silu(gate_ref[0, :, c].astype(F32))).astype(o_ref.dtype)


def _hgrn2(u, lower_bounds, norm_g, layer, t, heads_per_step):
    b, s, six_w = u.shape
    width = six_w // 6
    d = HGRN_HEAD_DIM
    gw = heads_per_step * d
    groups = width // gw
    blk = HGRN_BLOCK
    assert t % blk == 0 and blk // HGRN_CHUNK <= BF16_SUBLANES
    pos = jnp.arange(blk, dtype=jnp.int32)
    chunk_id = pos // HGRN_CHUNK
    tril = (chunk_id[:, None] == chunk_id[None, :]) & (pos[None, :] <= pos[:, None])
    select = jnp.arange(BF16_SUBLANES, dtype=jnp.int32)[:, None] == chunk_id[None, :]
    cum_m = jnp.concatenate([tril, select], axis=0).astype(BF16)
    col = lambda base: (lambda bi, gi, si: (bi, si, base * groups + gi))
    blocks = 4 * _nbytes((t, gw), u.dtype) + _nbytes((t, gw), BF16)
    resident = (2 * _nbytes(cum_m.shape, BF16) + _nbytes((heads_per_step, d, d), F32)
                + heads_per_step * (16 * _nbytes((t, d), F32) + 2 * _nbytes((t, blk), F32)))
    return pl.pallas_call(
        functools.partial(_hgrn_kernel, layer=layer, heads_per_step=heads_per_step),
        grid=(b, groups, s // t),
        in_specs=[
            pl.BlockSpec((1, t, gw), col(2)),
            pl.BlockSpec((1, t, gw), col(3)),
            pl.BlockSpec((1, t, gw), col(4)),
            pl.BlockSpec((1, t, gw), col(5)),
            pl.BlockSpec((lower_bounds.shape[0], gw), lambda bi, gi, si: (0, gi)),
            pl.BlockSpec((1, gw), lambda bi, gi, si: (0, gi)),
            pl.BlockSpec(cum_m.shape, lambda bi, gi, si: (0, 0)),
        ],
        out_specs=pl.BlockSpec((1, t, gw), lambda bi, gi, si: (bi, si, gi)),
        out_shape=jax.ShapeDtypeStruct((b, s, width), BF16),
        scratch_shapes=[pltpu.VMEM((heads_per_step, d, d), F32)],
        compiler_params=pltpu.CompilerParams(
            dimension_semantics=("arbitrary", "arbitrary", "arbitrary"),
            vmem_limit_bytes=_vmem_limit(blocks, resident)),
        name="hgrn2",
    )(u, u, u, u, lower_bounds, norm_g.reshape(1, width), cum_m)


def _sb_weights(tiles, later_m):
    nt_dims = (((1,), (1,)), ((), ()))
    z = [lax.dot_general(q, kb, nt_dims, preferred_element_type=F32) for q, kb, _, _ in tiles]
    log_beta, log_keep_b, carry = [], [], []
    for x, (_, _, c, mask) in zip(z, tiles):
        if mask is not None:
            x = jnp.where(mask, x, SB_MASKED_LOGIT)
        soft = jnp.log(1.0 + jnp.exp2(-jnp.abs(x))) * LOG2E
        lb = jnp.minimum(x, 0.0) - soft
        lk = lb - x
        log_beta.append(lb)
        log_keep_b.append(lk.astype(BF16))
        carry.append(c + jnp.sum(lk, axis=-1, keepdims=True))
    later = [jnp.dot(lk, later_m, preferred_element_type=F32) for lk in log_keep_b]
    a = [jnp.exp2(lb + lt + t[2]).astype(BF16) for lb, lt, t in zip(log_beta, later, tiles)]
    return list(zip(carry, a))


def _sb_kernel(q_ref, k_ref, v_ref, gate_ref, m_ref, o_ref, *, n_sub, tr):
    qi = pl.program_id(2)
    later_wide = m_ref[...]
    later_m = later_wide[:tr, :tr]
    col = lax.broadcasted_iota(jnp.int32, (tr, 2 * tr), 1)
    row = lax.broadcasted_iota(jnp.int32, (tr, 2 * tr), 0)
    causal_window = col - tr < row
    causal_start = col < row

    def rows(j, n):
        return pl.ds(pl.multiple_of(j * tr, tr), n * tr)

    def write(out, acc):
        o_ref[0, out, :] = (acc * _silu(gate_ref[0, out, :].astype(F32))).astype(o_ref.dtype)

    def step(sequence_start):
        half = tr // 2
        qs = [q_ref[0, s * tr:(s + 1) * tr, :] for s in range(n_sub)]
        blk = [qi * n_sub + s for s in range(n_sub)]
        no_carry = jnp.zeros((tr, 1), F32)
        start = [0 if (sequence_start and s == 0) else blk[s] - 1 for s in range(n_sub)]
        first = _sb_weights(
            [(qs[s], k_ref[0, rows(start[s], 2), :], no_carry,
              causal_start if (sequence_start and s == 0) else causal_window)
             for s in range(n_sub)], later_wide)
        with_second = [s for s in range(n_sub) if not (sequence_start and s < 2)]
        second = dict(zip(with_second, _sb_weights(
            [(qs[s][:half], k_ref[0, rows(blk[s] - 2, 1), :], first[s][0][:half], None)
             for s in with_second], later_m)))
        state = []
        for s in range(n_sub):
            carry, weights = first[s]
            if s in second:
                carry_top, a_far = second[s]
                top = jnp.dot(jnp.concatenate([a_far, weights[:half]], axis=1),
                              v_ref[0, rows(blk[s] - 2, 3), :], preferred_element_type=F32)
                bottom = jnp.dot(weights[half:], v_ref[0, rows(start[s], 2), :],
                                 preferred_element_type=F32)
                state.append((qs[s][:half], slice(s * tr, s * tr + half), blk[s] - 3,
                              carry_top, top))
                state.append((qs[s][half:], slice(s * tr + half, (s + 1) * tr), blk[s] - 2,
                              carry[half:], bottom))
            else:
                acc = jnp.dot(weights, v_ref[0, rows(start[s], 2), :], preferred_element_type=F32)
                state.append((qs[s], slice(s * tr, (s + 1) * tr), -1, carry, acc))
        for _, out, _, _, acc in state:
            write(out, acc)

        worst = jnp.max(jnp.concatenate([c for _, _, _, c, _ in state], axis=0))

        @pl.when(worst > SB_DEAD_LOG2_WEIGHT)
        def _():
            for q, out, nxt, carry, acc in state:
                def cond(st):
                    j, alive, _, _ = st
                    return jnp.logical_and(j >= 0, alive)

                def body(st, q=q):
                    j, _, carry, acc = st
                    ((carry, a),) = _sb_weights([(q, k_ref[0, rows(j, 1), :], carry, None)], later_m)
                    acc = acc + jnp.dot(a, v_ref[0, rows(j, 1), :], preferred_element_type=F32)
                    return j - 1, jnp.max(carry) > SB_DEAD_LOG2_WEIGHT, carry, acc

                alive = jnp.max(carry) > SB_DEAD_LOG2_WEIGHT
                _, _, _, acc = lax.while_loop(cond, body, (nxt, alive, carry, acc))
                write(out, acc)

    @pl.when(qi == 0)
    def _():
        step(True)

    @pl.when(qi > 0)
    def _():
        step(False)


def _stick_breaking(u, b, tq, tr):
    four_h, rows, d = u.shape
    heads = four_h // 4
    dm = heads * d
    s = rows // b
    nq = s // tq
    assert s >= 2 * tr and tq % tr == 0
    row = jnp.arange(2 * tr)
    later_m = (row[:, None] > row[None, :]).astype(BF16)
    blocks = 2 * _nbytes((s, d), BF16) + 3 * _nbytes((tq, d), BF16)
    resident = _nbytes(later_m.shape, BF16) + (tq // tr) * 12 * _nbytes((tr, 2 * tr), F32)
    return pl.pallas_call(
        functools.partial(_sb_kernel, n_sub=tq // tr, tr=tr),
        grid=(b, heads, nq),
        in_specs=[
            pl.BlockSpec((1, tq, d), lambda bi, hi, qi: (hi, bi * nq + qi, 0)),
            pl.BlockSpec((1, s, d), lambda bi, hi, qi: (heads + hi, bi, 0)),
            pl.BlockSpec((1, s, d), lambda bi, hi, qi: (2 * heads + hi, bi, 0)),
            pl.BlockSpec((1, tq, d), lambda bi, hi, qi: (3 * heads + hi, bi * nq + qi, 0)),
            pl.BlockSpec(later_m.shape, lambda bi, hi, qi: (0, 0)),
        ],
        out_specs=pl.BlockSpec((1, tq, d), lambda bi, hi, qi: (bi, qi, hi)),
        out_shape=jax.ShapeDtypeStruct((b, s, dm), BF16),
        compiler_params=pltpu.CompilerParams(
            dimension_semantics=("arbitrary", "arbitrary", "arbitrary"),
            vmem_limit_bytes=_vmem_limit(blocks, resident)),
        name="stick_breaking",
    )(u, u, u, u, later_m)


def _xattn_kernel(q_ref, k_ref, v_ref, o_ref, *, heads):
    dm = q_ref.shape[2]
    hd = dm // heads
    scale = 1.0 / math.sqrt(hd)
    cols = [slice(h * hd, (h + 1) * hd) for h in range(heads)]
    s = [lax.dot_general(q_ref[0, :, c], k_ref[0, :, c], (((1,), (1,)), ((), ())),
                         preferred_element_type=F32) * scale for c in cols]
    e = [jnp.exp(x - jnp.max(x, axis=-1, keepdims=True)) for x in s]
    p = [(x * (1.0 / jnp.sum(x, axis=-1, keepdims=True))).astype(BF16) for x in e]
    o = [jnp.dot(x, v_ref[0, :, c], preferred_element_type=F32) for x, c in zip(p, cols)]
    for x, c in zip(o, cols):
        o_ref[0, :, c] = x.astype(o_ref.dtype)


def _cross_attention_core(q, kv, t):
    b, s, dm = q.shape
    mlen = kv.shape[1]
    blocks = 2 * _nbytes((t, dm), BF16) + 2 * _nbytes((mlen, dm), BF16)
    resident = 6 * _nbytes((t, mlen), F32) + 2 * _nbytes((t, dm // XA_HEADS), F32)
    return pl.pallas_call(
        functools.partial(_xattn_kernel, heads=XA_HEADS),
        grid=(b, s // t),
        in_specs=[
            pl.BlockSpec((1, t, dm), lambda bi, si: (bi, si, 0)),
            pl.BlockSpec((1, mlen, dm), lambda bi, si: (bi, 0, 0)),
            pl.BlockSpec((1, mlen, dm), lambda bi, si: (bi, 0, 1)),
        ],
        out_specs=pl.BlockSpec((1, t, dm), lambda bi, si: (bi, si, 0)),
        out_shape=jax.ShapeDtypeStruct((b, s, dm), BF16),
        compiler_params=pltpu.CompilerParams(
            dimension_semantics=("arbitrary", "arbitrary"),
            vmem_limit_bytes=_vmem_limit(blocks, resident)),
        name="cross_attention",
    )(q, kv, kv)


MM_TM = 1024
MM_TN = 1024
MM_TN_BF16_OUT = 2048
RES_TM = 512
POOL_ROWS = 1024
HGRN_ROWS = 512
HGRN_BLOCK = 256
HGRN_HEADS_PER_STEP = 4
SB_Q_ROWS = 2048
SB_K_ROWS = 128
XA_ROWS = 2048


def kernel(x, mem, norm_mix_g, norm_xa_g, norm_mem_g, final_norm_g, w_in_even, pool_w,
           pool_scale, hgrn_lower_bounds, hgrn_norm_g, w_out_even, w_in_odd, w_out_odd,
           xa_wq, xa_wkv, xa_wo):
    b, s, dm = x.shape
    mlen = mem.shape[1]
    depth = norm_mix_g.shape[0]
    rows = b * s
    xs = x.reshape(rows, dm)
    mem2 = mem.reshape(b * mlen, dm)
    sb_col_scale = jnp.concatenate([jnp.full((dm,), LOG2E / math.sqrt(SB_HEAD_DIM), F32),
                                    jnp.ones((3 * dm,), F32)])
    w_in_even_b, w_out_even_b = w_in_even.astype(BF16), w_out_even.astype(BF16)
    w_in_odd_b, w_out_odd_b = (w_in_odd * sb_col_scale).astype(BF16), w_out_odd.astype(BF16)
    wq_b, wo_b = xa_wq.astype(BF16), xa_wo.astype(BF16)
    h = None
    for l in range(depth):
        if l % 2 == 0:
            e = l // 2
            if h is None:
                u = _norm_matmul(xs, norm_mix_g[l], w_in_even_b, e, BF16, MM_TM, MM_TN_BF16_OUT)
            else:
                u = _matmul(h, w_in_even_b, e, BF16, MM_TM, MM_TN_BF16_OUT)
            u = u.reshape(b, s, -1)
            y_a = _pool_mixer(u, pool_w[e].astype(BF16), pool_scale[e], POOL_ROWS)
            y_b = _hgrn2(u, hgrn_lower_bounds, hgrn_norm_g[e], e, HGRN_ROWS,
                         HGRN_HEADS_PER_STEP)
            mixed = [y_a.reshape(rows, -1), y_b.reshape(rows, -1)]
            w_out, w_out_layer = w_out_even_b, e
        else:
            o = l // 2
            assert h is not None, "stick-breaking layers follow a layer that emits h"
            u = _matmul(h, w_in_odd_b, o, BF16, MM_TM, MM_TN_BF16_OUT, head_dim=SB_HEAD_DIM)
            mixed = [_stick_breaking(u, b, SB_Q_ROWS, SB_K_ROWS).reshape(rows, dm)]
            w_out, w_out_layer = w_out_odd_b, o
        xs, h_xa = _matmul_residual(mixed, w_out, w_out_layer, xs, norm_xa_g[l], RES_TM, "emit")
        q = _matmul(h_xa, wq_b, l, BF16, MM_TM, MM_TN_BF16_OUT)
        kv = _norm_matmul(mem2, norm_mem_g[l], xa_wkv, l, BF16, b * mlen, MM_TN)
        att = _cross_attention_core(q.reshape(b, s, dm), kv.reshape(b, mlen, 2 * dm), XA_ROWS)
        att = [att.reshape(rows, dm)]
        if l == depth - 1:
            (xs,) = _matmul_residual(att, wo_b, l, xs, final_norm_g, RES_TM, "final")
        else:
            xs, h = _matmul_residual(att, wo_b, l, xs, norm_mix_g[l + 1], RES_TM, "emit")
    return xs.reshape(b, s, dm)
```
